```python
import jax, jax.numpy as jnp
from jax import lax
import numpy as np

D_MODEL = 1024
BATCH = 16
SEQ = 2048
DEPTH = 4

N_MIXERS = 2
N_MLA_LAYERS = (DEPTH + 1) // 2
N_GMLP_LAYERS = DEPTH // 2
MLA_HEADS = 8
QK_NOPE_DIM = 128
QK_ROPE_DIM = 64
V_HEAD_DIM = 128
Q_LORA_RANK = 384
KV_LORA_RANK = 256
ROPE_BASE = 10000.0
Q_BLOCK = 128
GMLP_CHUNK = 128
GMLP_HALF = 2 * D_MODEL
GMLP_GROUPS = 8
GMLP_GROUP_DIM = GMLP_HALF // GMLP_GROUPS
D_FF = 4 * D_MODEL
PLE_DIM = 256
NORM_EPS = 1e-6
MAX_POS_OFFSET = 4096

kernel_name = "hybrid_mla_chunked_gmlp_trunk"


def rms_norm(x, g):
    xf = x.astype(jnp.float32)
    y = xf * lax.rsqrt(jnp.mean(xf * xf, axis=-1, keepdims=True) + NORM_EPS)
    return (y * g.astype(jnp.float32)).astype(x.dtype)


def layer_norm(x, g, b):
    xf = x.astype(jnp.float32)
    mu = jnp.mean(xf, axis=-1, keepdims=True)
    xc = xf - mu
    y = xc * lax.rsqrt(jnp.mean(xc * xc, axis=-1, keepdims=True) + NORM_EPS)
    return (y * g.astype(jnp.float32) + b.astype(jnp.float32)).astype(x.dtype)


def rope_cos_sin(positions):
    inv_freq = ROPE_BASE ** (-(jnp.arange(0, QK_ROPE_DIM, 2, dtype=jnp.float32) / QK_ROPE_DIM))
    ang = positions.astype(jnp.float32)[..., None] * inv_freq
    return jnp.cos(ang), jnp.sin(ang)


def apply_rope(x, cos, sin):
    x1, x2 = jnp.split(x.astype(jnp.float32), 2, axis=-1)
    return jnp.concatenate([x1 * cos - x2 * sin, x2 * cos + x1 * sin], axis=-1).astype(x.dtype)


def mla_mixer(hn, cos, sin, w_down, q_lora_g, kv_lora_g, w_uq, w_ukv,
              q_nope_g, q_rope_g, k_nope_g, k_rope_g, w_out):
    B, S, _ = hn.shape
    H = MLA_HEADS
    lat = hn @ w_down
    c_q, c_kv, k_rope = jnp.split(lat, [Q_LORA_RANK, Q_LORA_RANK + KV_LORA_RANK], axis=-1)
    c_q = rms_norm(c_q, q_lora_g)
    c_kv = rms_norm(c_kv, kv_lora_g)
    q = (c_q @ w_uq).reshape(B, S, H, QK_NOPE_DIM + QK_ROPE_DIM)
    q_nope, q_rope = jnp.split(q, [QK_NOPE_DIM], axis=-1)
    kv = (c_kv @ w_ukv).reshape(B, S, H, QK_NOPE_DIM + V_HEAD_DIM)
    k_nope, v = jnp.split(kv, [QK_NOPE_DIM], axis=-1)
    q_nope = rms_norm(q_nope, q_nope_g)
    q_rope = apply_rope(rms_norm(q_rope, q_rope_g), cos[:, :, None], sin[:, :, None])
    k_nope = rms_norm(k_nope, k_nope_g)
    k_rope = apply_rope(rms_norm(k_rope, k_rope_g), cos, sin)
    scale = (QK_NOPE_DIM + QK_ROPE_DIM) ** -0.5
    outs = []
    for j in range(S // Q_BLOCK):
        q0 = j * Q_BLOCK
        kend = q0 + Q_BLOCK
        s = (jnp.einsum('bqhd,bkhd->bhqk', q_nope[:, q0:kend], k_nope[:, :kend])
             + jnp.einsum('bqhr,bkr->bhqk', q_rope[:, q0:kend], k_rope[:, :kend]))
        s = s.astype(jnp.float32) * scale
        causal = jnp.arange(kend)[None, :] <= (q0 + jnp.arange(Q_BLOCK))[:, None]
        s = jnp.where(causal, s, -jnp.inf)
        pr = jax.nn.softmax(s, axis=-1).astype(v.dtype)
        outs.append(jnp.einsum('bhqk,bkhd->bqhd', pr, v[:, :kend]))
    o = jnp.concatenate(outs, axis=1).reshape(B, S, H * V_HEAD_DIM)
    return o @ w_out


def gmlp_mixer(hn, w_in, ln_g, ln_b, w_s, b_s, w_out):
    B, S, _ = hn.shape
    z = jax.nn.gelu(hn @ w_in)
    u, v = jnp.split(z, 2, axis=-1)
    v = layer_norm(v, ln_g, ln_b)
    v = v.reshape(B, S // GMLP_CHUNK, GMLP_CHUNK, GMLP_GROUPS, GMLP_GROUP_DIM)
    mask = jnp.tril(jnp.ones((GMLP_CHUNK, GMLP_CHUNK), dtype=w_s.dtype))
    ws = w_s * mask
    sv = jnp.einsum('gts,bnsgd->bntgd', ws, v) + b_s.T[None, None, :, :, None]
    y = u * sv.reshape(B, S, GMLP_HALF)
    return y @ w_out


def setup_inputs(seed: int = 0) -> dict:
    key = jax.random.key(seed)
    ks = iter(jax.random.split(key, 40))

    def nrm(shape, scale):
        return jax.random.normal(next(ks), shape, dtype=jnp.float32) * scale

    def gain(shape):
        return 1.0 + nrm(shape, 0.02)

    nA, nB, D = N_MLA_LAYERS, N_GMLP_LAYERS, D_MODEL
    x = nrm((BATCH, SEQ, D), 1.0)
    p = nrm((DEPTH, BATCH, SEQ, PLE_DIM), 1.0)
    offs = jax.random.randint(next(ks), (BATCH, 1), 0, MAX_POS_OFFSET, dtype=jnp.int32)
    positions = offs + jnp.arange(SEQ, dtype=jnp.int32)[None, :]
    down_w = Q_LORA_RANK + KV_LORA_RANK + QK_ROPE_DIM
    return {
        "x": x,
        "p": p,
        "positions": positions,
        "norm_mix": gain((DEPTH, D)),
        "norm_ffn": gain((DEPTH, D)),
        "norm_ple": gain((DEPTH, D)),
        "mla_w_down": nrm((nA, D, down_w), D ** -0.5),
        "mla_q_lora_g": gain((nA, Q_LORA_RANK)),
        "mla_kv_lora_g": gain((nA, KV_LORA_RANK)),
        "mla_w_uq": nrm((nA, Q_LORA_RANK, MLA_HEADS * (QK_NOPE_DIM + QK_ROPE_DIM)), Q_LORA_RANK ** -0.5),
        "mla_w_ukv": nrm((nA, KV_LORA_RANK, MLA_HEADS * (QK_NOPE_DIM + V_HEAD_DIM)), KV_LORA_RANK ** -0.5),
        "mla_q_nope_g": gain((nA, QK_NOPE_DIM)),
        "mla_q_rope_g": gain((nA, QK_ROPE_DIM)),
        "mla_k_nope_g": gain((nA, QK_NOPE_DIM)),
        "mla_k_rope_g": gain((nA, QK_ROPE_DIM)),
        "mla_w_out": nrm((nA, MLA_HEADS * V_HEAD_DIM, D), 0.5 * (MLA_HEADS * V_HEAD_DIM) ** -0.5),
        "gmlp_w_in": nrm((nB, D, 2 * GMLP_HALF), D ** -0.5),
        "gmlp_ln_g": gain((nB, GMLP_HALF)),
        "gmlp_ln_b": nrm((nB, GMLP_HALF), 0.02),
        "gmlp_w_s": nrm((nB, GMLP_GROUPS, GMLP_CHUNK, GMLP_CHUNK), GMLP_CHUNK ** -0.5),
        "gmlp_b_s": 1.0 + nrm((nB, GMLP_GROUPS, GMLP_CHUNK), 0.1),
        "gmlp_w_out": nrm((nB, GMLP_HALF, D), 0.5 * GMLP_HALF ** -0.5),
        "ffn_w_up": nrm((DEPTH, D, D_FF), D ** -0.5),
        "ffn_w_down": nrm((DEPTH, D_FF, D), 0.5 * D_FF ** -0.5),
        "ple_w_gate": nrm((DEPTH, D, D), D ** -0.5),
        "ple_w_proj": nrm((DEPTH, PLE_DIM, D), PLE_DIM ** -0.5),
    }


def reference(x, p, positions, norm_mix, norm_ffn, norm_ple,
              mla_w_down, mla_q_lora_g, mla_kv_lora_g, mla_w_uq, mla_w_ukv,
              mla_q_nope_g, mla_q_rope_g, mla_k_nope_g, mla_k_rope_g, mla_w_out,
              gmlp_w_in, gmlp_ln_g, gmlp_ln_b, gmlp_w_s, gmlp_b_s, gmlp_w_out,
              ffn_w_up, ffn_w_down, ple_w_gate, ple_w_proj):
    cos, sin = rope_cos_sin(positions)
    h = x
    for i in range(DEPTH):
        hn = rms_norm(h, norm_mix[i])
        j = i // N_MIXERS
        if i % N_MIXERS == 0:
            mix = mla_mixer(hn, cos, sin, mla_w_down[j], mla_q_lora_g[j], mla_kv_lora_g[j],
                            mla_w_uq[j], mla_w_ukv[j], mla_q_nope_g[j], mla_q_rope_g[j],
                            mla_k_nope_g[j], mla_k_rope_g[j], mla_w_out[j])
        else:
            mix = gmlp_mixer(hn, gmlp_w_in[j], gmlp_ln_g[j], gmlp_ln_b[j],
                             gmlp_w_s[j], gmlp_b_s[j], gmlp_w_out[j])
        h = h + mix
        hn = rms_norm(h, norm_ffn[i])
        h = h + jnp.square(jax.nn.relu(hn @ ffn_w_up[i])) @ ffn_w_down[i]
        hn = rms_norm(h, norm_ple[i])
        h = h + jax.nn.sigmoid(hn @ ple_w_gate[i]) * (p[i] @ ple_w_proj[i])
    return h
```

```python
import functools

import jax
import jax.numpy as jnp
from jax import lax
from jax.experimental import pallas as pl
from jax.experimental.pallas import tpu as pltpu

D_MODEL = 1024
DEPTH = 4
MLA_HEADS = 8
QK_NOPE_DIM = 128
QK_ROPE_DIM = 64
V_HEAD_DIM = 128
Q_LORA_RANK = 384
KV_LORA_RANK = 256
ROPE_BASE = 10000.0
GMLP_CHUNK = 128
GMLP_HALF = 2 * D_MODEL
GMLP_GROUPS = 8
GMLP_GROUP_DIM = GMLP_HALF // GMLP_GROUPS
D_FF = 4 * D_MODEL
PLE_DIM = 256
NORM_EPS = 1e-6

V7X_LANES = 128
V7X_VMEM_REQUEST_CAP = 56 << 20

ROPE_HALF = QK_ROPE_DIM // 2
ROPE_PAD = V7X_LANES
QK_PAD_DIM = QK_NOPE_DIM + ROPE_PAD
LAT_PAD_DIM = Q_LORA_RANK + KV_LORA_RANK + ROPE_PAD

PROJ_ROWS = 512
ATTN_Q_ROWS = 256
POST_ROWS = 512
FF_CHUNK = 1024
GMLP_ROWS = 256

_BF16 = jnp.bfloat16
_F32 = jnp.float32


def _vmem_limit(resident_bytes, streamed_bytes, scratch_bytes):
    need = resident_bytes + 2 * streamed_bytes + scratch_bytes
    return min(V7X_VMEM_REQUEST_CAP, need)


def _nbytes(shape, dtype):
    n = 1
    for s in shape:
        n *= s
    return n * jnp.dtype(dtype).itemsize


def _resident(shape):
    zeros = (0,) * len(shape)
    return pl.BlockSpec(shape, lambda *_: zeros, pipeline_mode=pl.Buffered(1))


def _dot(a, b):
    return jnp.dot(a, b, preferred_element_type=_F32)


def _rms_scale(x, n):
    ss = jnp.sum(x * x, axis=-1, keepdims=True)
    return lax.rsqrt(ss / n + NORM_EPS)


def _mla_proj_kernel(h_ref, pos_ref, gmix_ref, wdown_ref, gql_ref, gkvl_ref, wuq_ref, wukv_ref,
                     gq_ref, gkn_ref, gkr_ref, invf_ref, sgn_ref, q_ref, k_ref, v_ref):
    h = h_ref[0]
    hn = (h * _rms_scale(h, D_MODEL) * gmix_ref[...]).astype(_BF16)
    lat = _dot(hn, wdown_ref[...])
    c_q = lat[:, :Q_LORA_RANK]
    c_kv = lat[:, Q_LORA_RANK:Q_LORA_RANK + KV_LORA_RANK]
    k_rope = lat[:, Q_LORA_RANK + KV_LORA_RANK:]
    c_q = (c_q * _rms_scale(c_q, Q_LORA_RANK) * gql_ref[...]).astype(_BF16)
    c_kv = (c_kv * _rms_scale(c_kv, KV_LORA_RANK) * gkvl_ref[...]).astype(_BF16)

    ang = pos_ref[0].astype(_F32) * invf_ref[...]
    cos = jnp.cos(ang)
    sin_signed = jnp.sin(ang) * sgn_ref[...]

    def rope(x):
        return x * cos + pltpu.roll(x, ROPE_PAD // 2, 1) * sin_signed

    k_rope = rope(k_rope * _rms_scale(k_rope, QK_ROPE_DIM) * gkr_ref[...]).astype(_BF16)

    q = _dot(c_q, wuq_ref[...])
    kv = _dot(c_kv, wukv_ref[...])
    gq = gq_ref[...]
    for hd in range(MLA_HEADS):
        base = hd * QK_PAD_DIM
        qn = q[:, base:base + QK_NOPE_DIM]
        qr = q[:, base + QK_NOPE_DIM:base + QK_PAD_DIM]
        qn = qn * _rms_scale(qn, QK_NOPE_DIM) * gq[:, :QK_NOPE_DIM]
        qr = rope(qr * _rms_scale(qr, QK_ROPE_DIM) * gq[:, QK_NOPE_DIM:])
        q_ref[0, hd, :, :QK_NOPE_DIM] = qn.astype(_BF16)
        q_ref[0, hd, :, QK_NOPE_DIM:] = qr.astype(_BF16)
        kn = kv[:, hd * QK_NOPE_DIM:(hd + 1) * QK_NOPE_DIM]
        kn = kn * _rms_scale(kn, QK_NOPE_DIM) * gkn_ref[...]
        k_ref[0, hd, :, :QK_NOPE_DIM] = kn.astype(_BF16)
        k_ref[0, hd, :, QK_NOPE_DIM:] = k_rope
        v0 = MLA_HEADS * QK_NOPE_DIM + hd * V_HEAD_DIM
        v_ref[0, hd] = kv[:, v0:v0 + V_HEAD_DIM].astype(_BF16)


def _mla_proj(h, pos, gmix, wdown, gql, gkvl, wuq, wukv, gq, gkn, gkr, invf, sgn):
    B, S, D = h.shape
    rows = PROJ_ROWS
    H = MLA_HEADS
    weights = (gmix, wdown, gql, gkvl, wuq, wukv, gq, gkn, gkr, invf, sgn)
    resident = sum(_nbytes(w.shape, w.dtype) for w in weights)
    streamed = (_nbytes((rows, D), _F32) + _nbytes((rows, V7X_LANES), jnp.int32)
                + 2 * _nbytes((H, rows, QK_PAD_DIM), _BF16) + _nbytes((H, rows, V_HEAD_DIM), _BF16))
    scratch = 4 * _nbytes((rows, H * QK_PAD_DIM), _F32)
    return pl.pallas_call(
        _mla_proj_kernel,
        grid=(B, S // rows),
        in_specs=[pl.BlockSpec((1, rows, D), lambda b, i: (b, i, 0)),
                  pl.BlockSpec((1, rows, 1), lambda b, i: (b, i, 0))]
                 + [_resident(w.shape) for w in weights],
        out_specs=[pl.BlockSpec((1, H, rows, QK_PAD_DIM), lambda b, i: (b, 0, i, 0)),
                   pl.BlockSpec((1, H, rows, QK_PAD_DIM), lambda b, i: (b, 0, i, 0)),
                   pl.BlockSpec((1, H, rows, V_HEAD_DIM), lambda b, i: (b, 0, i, 0))],
        out_shape=[jax.ShapeDtypeStruct((B, H, S, QK_PAD_DIM), _BF16),
                   jax.ShapeDtypeStruct((B, H, S, QK_PAD_DIM), _BF16),
                   jax.ShapeDtypeStruct((B, H, S, V_HEAD_DIM), _BF16)],
        compiler_params=pltpu.CompilerParams(
            dimension_semantics=("parallel", "parallel"),
            vmem_limit_bytes=_vmem_limit(resident, streamed, scratch)),
        name="mla_proj",
    )(h, pos, *weights)


def _attn_kernel(q_ref, k_ref, v_ref, o_ref):
    S = q_ref.shape[2]
    tq = ATTN_Q_ROWS
    row = lax.broadcasted_iota(jnp.int32, (tq, tq), 0)
    col = lax.broadcasted_iota(jnp.int32, (tq, tq), 1)
    diag_mask = col <= row
    for j in range(S // tq):
        q0, kend = j * tq, (j + 1) * tq
        q = q_ref[0, 0, q0:kend, :]
        s_diag = lax.dot_general(q, k_ref[0, 0, q0:kend, :], (((1,), (1,)), ((), ())),
                                 preferred_element_type=_F32)
        s_diag = jnp.where(diag_mask, s_diag, -jnp.inf)
        if j:
            s_past = lax.dot_general(q, k_ref[0, 0, :q0, :], (((1,), (1,)), ((), ())),
                                     preferred_element_type=_F32)
            m = jnp.maximum(jnp.max(s_past, axis=-1, keepdims=True),
                            jnp.max(s_diag, axis=-1, keepdims=True))
            p_past = jnp.exp(s_past - m)
            p_diag = jnp.exp(s_diag - m)
            l = jnp.sum(p_past, axis=-1, keepdims=True) + jnp.sum(p_diag, axis=-1, keepdims=True)
            o = (_dot(p_past.astype(_BF16), v_ref[0, 0, :q0, :])
                 + _dot(p_diag.astype(_BF16), v_ref[0, 0, q0:kend, :]))
        else:
            m = jnp.max(s_diag, axis=-1, keepdims=True)
            p_diag = jnp.exp(s_diag - m)
            l = jnp.sum(p_diag, axis=-1, keepdims=True)
            o = _dot(p_diag.astype(_BF16), v_ref[0, 0, q0:kend, :])
        o_ref[0, q0:kend, :] = (o / l).astype(_BF16)


def _attention(q, k, v):
    B, H, S, _ = q.shape
    streamed = (2 * _nbytes((S, QK_PAD_DIM), _BF16) + 2 * _nbytes((S, V_HEAD_DIM), _BF16))
    scratch = 8 * _nbytes((ATTN_Q_ROWS, S), _F32)
    return pl.pallas_call(
        _attn_kernel,
        grid=(B, H),
        in_specs=[pl.BlockSpec((1, 1, S, QK_PAD_DIM), lambda b, h: (b, h, 0, 0)),
                  pl.BlockSpec((1, 1, S, QK_PAD_DIM), lambda b, h: (b, h, 0, 0)),
                  pl.BlockSpec((1, 1, S, V_HEAD_DIM), lambda b, h: (b, h, 0, 0))],
        out_specs=pl.BlockSpec((1, S, V_HEAD_DIM), lambda b, h: (b, 0, h)),
        out_shape=jax.ShapeDtypeStruct((B, S, H * V_HEAD_DIM), _BF16),
        compiler_params=pltpu.CompilerParams(
            dimension_semantics=("parallel", "parallel"),
            vmem_limit_bytes=_vmem_limit(0, streamed, scratch)),
        name="mla_attention",
    )(q, k, v)


def _post_kernel(mix_ref, h_ref, p_ref, wmix_ref, gffn_ref, wup_ref, wdn_ref, gple_ref,
                 wgate_ref, wproj_ref, out_ref):
    h = h_ref[...] + _dot(mix_ref[...], wmix_ref[...])
    hn = (h * _rms_scale(h, D_MODEL) * gffn_ref[...]).astype(_BF16)
    for c in range(D_FF // FF_CHUNK):
        up = _dot(hn, wup_ref[:, c * FF_CHUNK:(c + 1) * FF_CHUNK])
        act = jnp.square(jnp.maximum(up, 0.0)).astype(_BF16)
        h = h + _dot(act, wdn_ref[c * FF_CHUNK:(c + 1) * FF_CHUNK, :])
    hn = (h * _rms_scale(h, D_MODEL) * gple_ref[...]).astype(_BF16)
    gate = jax.nn.sigmoid(_dot(hn, wgate_ref[...]))
    proj = _dot(p_ref[...].astype(_BF16), wproj_ref[...])
    out_ref[...] = h + gate * proj


def _post(mix, h, p, wmix, gffn, wup, wdn, gple, wgate, wproj):
    T, D = h.shape
    rows = POST_ROWS
    kmix = mix.shape[1]
    weights = (wmix, gffn, wup, wdn, gple, wgate, wproj)
    resident = sum(_nbytes(w.shape, w.dtype) for w in weights)
    streamed = (_nbytes((rows, kmix), _BF16) + 2 * _nbytes((rows, D), _F32)
                + _nbytes((rows, PLE_DIM), _F32))
    scratch = 2 * _nbytes((rows, FF_CHUNK), _F32) + 4 * _nbytes((rows, D), _F32)
    return pl.pallas_call(
        _post_kernel,
        grid=(T // rows,),
        in_specs=[pl.BlockSpec((rows, kmix), lambda i: (i, 0)),
                  pl.BlockSpec((rows, D), lambda i: (i, 0)),
                  pl.BlockSpec((rows, PLE_DIM), lambda i: (i, 0))]
                 + [_resident(w.shape) for w in weights],
        out_specs=pl.BlockSpec((rows, D), lambda i: (i, 0)),
        out_shape=jax.ShapeDtypeStruct((T, D), _F32),
        compiler_params=pltpu.CompilerParams(
            dimension_semantics=("parallel",),
            vmem_limit_bytes=_vmem_limit(resident, streamed, scratch)),
        name="post_ffn_ple",
    )(mix, h, p, *weights)


def _gelu_tanh(x):
    return 0.5 * x * (1.0 + jnp.tanh(0.7978845608028654 * (x + 0.044715 * (x * x * x))))


def _gmlp_kernel(h_ref, gmix_ref, win_ref, lng_ref, lnb_ref, ws_ref, bst_ref, y_ref):
    h = h_ref[...]
    rows = h.shape[0]
    hn = (h * _rms_scale(h, D_MODEL) * gmix_ref[...]).astype(_BF16)
    v = _gelu_tanh(_dot(hn, win_ref[:, GMLP_HALF:]))
    mu = jnp.mean(v, axis=-1, keepdims=True)
    vc = v - mu
    var = jnp.mean(vc * vc, axis=-1, keepdims=True)
    vn = (vc * lax.rsqrt(var + NORM_EPS) * lng_ref[...] + lnb_ref[...]).astype(_BF16)
    u = _gelu_tanh(_dot(hn, win_ref[:, :GMLP_HALF]))
    t_idx = lax.broadcasted_iota(jnp.int32, (GMLP_CHUNK, GMLP_CHUNK), 0)
    s_idx = lax.broadcasted_iota(jnp.int32, (GMLP_CHUNK, GMLP_CHUNK), 1)
    causal = s_idx <= t_idx
    for g in range(GMLP_GROUPS):
        ws = jnp.where(causal, ws_ref[g], 0.0).astype(_BF16)
        bias = bst_ref[:, g:g + 1]
        c0 = g * GMLP_GROUP_DIM
        for n in range(rows // GMLP_CHUNK):
            r0 = n * GMLP_CHUNK
            sv = _dot(ws, vn[r0:r0 + GMLP_CHUNK, c0:c0 + GMLP_GROUP_DIM]) + bias
            y_ref[r0:r0 + GMLP_CHUNK, c0:c0 + GMLP_GROUP_DIM] = (
                u[r0:r0 + GMLP_CHUNK, c0:c0 + GMLP_GROUP_DIM] * sv).astype(_BF16)


def _gmlp_front(h, gmix, win, lng, lnb, ws, bst):
    T, D = h.shape
    rows = GMLP_ROWS
    weights = (gmix, win, lng, lnb, ws, bst)
    resident = sum(_nbytes(w.shape, w.dtype) for w in weights)
    streamed = _nbytes((rows, D), _F32) + _nbytes((rows, GMLP_HALF), _BF16)
    scratch = 6 * _nbytes((rows, GMLP_HALF), _F32)
    return pl.pallas_call(
        _gmlp_kernel,
        grid=(T // rows,),
        in_specs=[pl.BlockSpec((rows, D), lambda i: (i, 0))] + [_resident(w.shape) for w in weights],
        out_specs=pl.BlockSpec((rows, GMLP_HALF), lambda i: (i, 0)),
        out_shape=jax.ShapeDtypeStruct((T, GMLP_HALF), _BF16),
        compiler_params=pltpu.CompilerParams(
            dimension_semantics=("parallel",),
            vmem_limit_bytes=_vmem_limit(resident, streamed, scratch)),
        name="gmlp_front",
    )(h, *weights)


def _rope_pad(w):
    z = jnp.zeros(w.shape[:-1] + (ROPE_HALF,), w.dtype)
    return jnp.concatenate([w[..., :ROPE_HALF], z, w[..., ROPE_HALF:], z], axis=-1)


def _row(g):
    return g.reshape(1, -1).astype(_F32)


def _mla_params(w_down, q_lora_g, kv_lora_g, w_uq, w_ukv, q_nope_g, q_rope_g, k_nope_g, k_rope_g):
    H = MLA_HEADS
    n_lat = Q_LORA_RANK + KV_LORA_RANK
    wdown = jnp.concatenate([w_down[:, :n_lat], _rope_pad(w_down[:, n_lat:])], axis=-1).astype(_BF16)
    uq = w_uq.reshape(Q_LORA_RANK, H, QK_NOPE_DIM + QK_ROPE_DIM)
    wuq = jnp.concatenate([uq[..., :QK_NOPE_DIM], _rope_pad(uq[..., QK_NOPE_DIM:])], axis=-1)
    wuq = wuq.reshape(Q_LORA_RANK, H * QK_PAD_DIM).astype(_BF16)
    ukv = w_ukv.reshape(KV_LORA_RANK, H, QK_NOPE_DIM + V_HEAD_DIM)
    wukv = jnp.concatenate([ukv[..., :QK_NOPE_DIM].reshape(KV_LORA_RANK, H * QK_NOPE_DIM),
                            ukv[..., QK_NOPE_DIM:].reshape(KV_LORA_RANK, H * V_HEAD_DIM)],
                           axis=-1).astype(_BF16)
    scale = (QK_NOPE_DIM + QK_ROPE_DIM) ** -0.5
    gq = _row(jnp.concatenate([q_nope_g, _rope_pad(q_rope_g)])) * scale
    return (wdown, _row(q_lora_g), _row(kv_lora_g), wuq, wukv, gq, _row(k_nope_g),
            _row(_rope_pad(k_rope_g)))


def kernel(x, p, positions, norm_mix, norm_ffn, norm_ple, mla_w_down, mla_q_lora_g, mla_kv_lora_g,
           mla_w_uq, mla_w_ukv, mla_q_nope_g, mla_q_rope_g, mla_k_nope_g, mla_k_rope_g, mla_w_out,
           gmlp_w_in, gmlp_ln_g, gmlp_ln_b, gmlp_w_s, gmlp_b_s, gmlp_w_out, ffn_w_up, ffn_w_down,
           ple_w_gate, ple_w_proj):
    B, S, D = x.shape
    T = B * S
    inv_freq = ROPE_BASE ** (-(jnp.arange(0, QK_ROPE_DIM, 2, dtype=_F32) / QK_ROPE_DIM))
    invf = _row(_rope_pad(jnp.concatenate([inv_freq, inv_freq])))
    ones = jnp.ones((ROPE_HALF,), _F32)
    sgn = _row(_rope_pad(jnp.concatenate([-ones, ones])))
    pos = positions.reshape(B, S, 1)

    h = x.reshape(T, D)
    for i in range(DEPTH):
        j = i // 2
        if i % 2 == 0:
            params = _mla_params(mla_w_down[j], mla_q_lora_g[j], mla_kv_lora_g[j], mla_w_uq[j],
                                 mla_w_ukv[j], mla_q_nope_g[j], mla_q_rope_g[j], mla_k_nope_g[j],
                                 mla_k_rope_g[j])
            wdown, gql, gkvl, wuq, wukv, gq, gkn, gkr = params
            q, k, v = _mla_proj(h.reshape(B, S, D), pos, _row(norm_mix[i]), wdown, gql, gkvl, wuq,
                                wukv, gq, gkn, gkr, invf, sgn)
            mix = _attention(q, k, v).reshape(T, MLA_HEADS * V_HEAD_DIM)
            wmix = mla_w_out[j].astype(_BF16)
        else:
            mix = _gmlp_front(h, _row(norm_mix[i]), gmlp_w_in[j].astype(_BF16), _row(gmlp_ln_g[j]),
                              _row(gmlp_ln_b[j]), gmlp_w_s[j], gmlp_b_s[j].T)
            wmix = gmlp_w_out[j].astype(_BF16)
        h = _post(mix, h, p[i].reshape(T, PLE_DIM), wmix, _row(norm_ffn[i]),
                  ffn_w_up[i].astype(_BF16), ffn_w_down[i].astype(_BF16), _row(norm_ple[i]),
                  ple_w_gate[i].astype(_BF16), ple_w_proj[i].astype(_BF16))
    return h.reshape(B, S, D)
```

```python
import math

import jax
import jax.numpy as jnp
from jax import lax
from jax.experimental import pallas as pl
from jax.experimental.pallas import tpu as pltpu

D_MODEL = 1024
DEPTH = 4
MLA_HEADS = 8
QK_NOPE_DIM = 128
QK_ROPE_DIM = 64
V_HEAD_DIM = 128
Q_LORA_RANK = 384
KV_LORA_RANK = 256
ROPE_BASE = 10000.0
GMLP_CHUNK = 128
GMLP_HALF = 2 * D_MODEL
GMLP_GROUPS = 8
GMLP_GROUP_DIM = GMLP_HALF // GMLP_GROUPS
D_FF = 4 * D_MODEL
PLE_DIM = 256
NORM_EPS = 1e-6

V7X_LANES = 128
V7X_VMEM_REQUEST_CAP = 56 << 20

ROPE_HALF = QK_ROPE_DIM // 2
ROPE_PAD = V7X_LANES
QK_PAD_DIM = QK_NOPE_DIM + ROPE_PAD
LAT_PAD_DIM = Q_LORA_RANK + KV_LORA_RANK + ROPE_PAD
V_EXT_DIM = 2 * V_HEAD_DIM

PROJ_ROWS = 512
PROJ_SUB_ROWS = 256
ATTN_Q_ROWS = 256
ATTN_HEADS_PER_STEP = 4
POST_ROWS = 512
FF_CHUNK = 1024
GMLP_ROWS = 512
GMLP_SUB_ROWS = 256
_BF16 = jnp.bfloat16
_F32 = jnp.float32


def _vmem_limit(resident_bytes, streamed_bytes, scratch_bytes):
    need = resident_bytes + 2 * streamed_bytes + scratch_bytes
    return min(V7X_VMEM_REQUEST_CAP, need)


def _nbytes(shape, dtype):
    return math.prod(shape) * jnp.dtype(dtype).itemsize


def _layer_spec(arr, layer):
    idx = (layer,) + (0,) * (arr.ndim - 1)
    return pl.BlockSpec((None,) + arr.shape[1:], lambda *_: idx, pipeline_mode=pl.Buffered(1))


def _slab_bytes(arrs):
    return sum(_nbytes(a.shape[1:], a.dtype) for a in arrs)


def _dot(a, b):
    return jnp.dot(a, b, preferred_element_type=_F32)


def _rms_scale(x, n):
    ss = jnp.sum(x * x, axis=-1, keepdims=True)
    return lax.rsqrt(ss / n + NORM_EPS)


def _mean_sq(x, m):
    return _dot((x * x).astype(_BF16), m)


def _lane_tile(r, width):
    return jnp.concatenate([r] * (width // r.shape[1]), axis=1)


def _rope_table_kernel(pos_ref, invf_ref, sgn_ref, cos_ref, sin_ref):
    ang = pos_ref[0].astype(_F32) * invf_ref[...]
    cos_ref[0] = jnp.cos(ang)
    sin_ref[0] = jnp.sin(ang) * sgn_ref[...]


def _rope_tables(pos, invf, sgn):
    B, S, _ = pos.shape
    rows = PROJ_ROWS
    out = jax.ShapeDtypeStruct((B, S, ROPE_PAD), _F32)
    table_spec = pl.BlockSpec((1, rows, ROPE_PAD), lambda b, i: (b, i, 0))
    const_spec = pl.BlockSpec((1, ROPE_PAD), lambda b, i: (0, 0))
    return pl.pallas_call(
        _rope_table_kernel,
        grid=(B, S // rows),
        in_specs=[pl.BlockSpec((1, rows, 1), lambda b, i: (b, i, 0)), const_spec, const_spec],
        out_specs=[table_spec, table_spec],
        out_shape=[out, out],
        compiler_params=pltpu.CompilerParams(dimension_semantics=("parallel", "parallel")),
        name="rope_tables",
    )(pos, invf, sgn)


def _mla_proj_rows(r0, h_ref, cos_ref, sin_ref, gmix_ref, wdown_ref, gql_ref, gkvl_ref, wuq_ref,
                   wukv_ref, gq_ref, gkn_ref, gkr_ref, mh_ref, mcq_ref, mckv_ref, mkr_ref, mq_ref,
                   mk_ref, q_ref, k_ref, v_ref):
    rs = slice(r0, r0 + PROJ_SUB_ROWS)
    h = h_ref[0, rs, :]
    r_h = lax.rsqrt(_mean_sq(h, mh_ref[...]) + NORM_EPS)
    hn = (h * _lane_tile(r_h, D_MODEL) * gmix_ref[...]).astype(_BF16)
    lat = _dot(hn, wdown_ref[...])
    c_q = lat[:, :Q_LORA_RANK]
    c_kv = lat[:, Q_LORA_RANK:Q_LORA_RANK + KV_LORA_RANK]
    k_rope = lat[:, Q_LORA_RANK + KV_LORA_RANK:]
    r_cq = lax.rsqrt(_mean_sq(c_q, mcq_ref[...]) * (1.0 / Q_LORA_RANK) + NORM_EPS)
    c_q = (c_q * _lane_tile(r_cq, Q_LORA_RANK) * gql_ref[...]).astype(_BF16)
    r_ckv = lax.rsqrt(_mean_sq(c_kv, mckv_ref[...]) + NORM_EPS)
    c_kv = (c_kv * _lane_tile(r_ckv, KV_LORA_RANK) * gkvl_ref[...]).astype(_BF16)

    cos = cos_ref[0, rs, :]
    sin_signed = sin_ref[0, rs, :]

    def rope(x):
        return x * cos + pltpu.roll(x, ROPE_PAD // 2, 1) * sin_signed

    r_kr = lax.rsqrt(_mean_sq(k_rope, mkr_ref[...]) + NORM_EPS)
    k_rope = rope(k_rope * r_kr * gkr_ref[...]).astype(_BF16)

    q = _dot(c_q, wuq_ref[...])
    kv = _dot(c_kv, wukv_ref[...])
    gq = gq_ref[...]
    gkn = gkn_ref[...]
    for hd in range(MLA_HEADS):
        xq = q[:, hd * QK_PAD_DIM:(hd + 1) * QK_PAD_DIM]
        yq = xq * lax.rsqrt(_mean_sq(xq, mq_ref[...]) + NORM_EPS) * gq
        q_ref[0, hd, rs, :QK_NOPE_DIM] = yq[:, :QK_NOPE_DIM].astype(_BF16)
        q_ref[0, hd, rs, QK_NOPE_DIM:] = rope(yq[:, QK_NOPE_DIM:]).astype(_BF16)
        k_ref[0, hd, rs, QK_NOPE_DIM:] = k_rope
        v0 = MLA_HEADS * QK_NOPE_DIM + hd * V_HEAD_DIM
        v_ref[0, hd, rs, :] = kv[:, v0:v0 + V_HEAD_DIM].astype(_BF16)
    for pair in range(MLA_HEADS // 2):
        xk = kv[:, 2 * pair * QK_NOPE_DIM:(2 * pair + 2) * QK_NOPE_DIM]
        yk = (xk * lax.rsqrt(_mean_sq(xk, mk_ref[...]) + NORM_EPS) * gkn).astype(_BF16)
        k_ref[0, 2 * pair, rs, :QK_NOPE_DIM] = yk[:, :QK_NOPE_DIM]
        k_ref[0, 2 * pair + 1, rs, :QK_NOPE_DIM] = yk[:, QK_NOPE_DIM:]


def _mla_proj_kernel(*refs):
    for r0 in range(0, PROJ_ROWS, PROJ_SUB_ROWS):
        _mla_proj_rows(r0, *refs)


def _mla_proj(h, cos, sin, layer_ops, shared_ops):
    B, S, D = h.shape
    rows = PROJ_ROWS
    H = MLA_HEADS
    ops = layer_ops + shared_ops
    resident = _slab_bytes([a for a, _ in ops])
    streamed = (_nbytes((rows, D), _F32) + 2 * _nbytes((rows, ROPE_PAD), _F32)
                + 2 * _nbytes((H, rows, QK_PAD_DIM), _BF16) + _nbytes((H, rows, V_HEAD_DIM), _BF16))
    scratch = 4 * _nbytes((rows, H * QK_PAD_DIM), _F32)
    return pl.pallas_call(
        _mla_proj_kernel,
        grid=(B, S // rows),
        in_specs=[pl.BlockSpec((1, rows, D), lambda b, i: (b, i, 0)),
                  pl.BlockSpec((1, rows, ROPE_PAD), lambda b, i: (b, i, 0)),
                  pl.BlockSpec((1, rows, ROPE_PAD), lambda b, i: (b, i, 0))]
                 + [_layer_spec(a, l) for a, l in ops],
        out_specs=[pl.BlockSpec((1, H, rows, QK_PAD_DIM), lambda b, i: (b, 0, i, 0)),
                   pl.BlockSpec((1, H, rows, QK_PAD_DIM), lambda b, i: (b, 0, i, 0)),
                   pl.BlockSpec((1, H, rows, V_HEAD_DIM), lambda b, i: (b, 0, i, 0))],
        out_shape=[jax.ShapeDtypeStruct((B, H, S, QK_PAD_DIM), _BF16),
                   jax.ShapeDtypeStruct((B, H, S, QK_PAD_DIM), _BF16),
                   jax.ShapeDtypeStruct((B, H, S, V_HEAD_DIM), _BF16)],
        compiler_params=pltpu.CompilerParams(
            dimension_semantics=("parallel", "parallel"),
            vmem_limit_bytes=_vmem_limit(resident, streamed, scratch)),
        name="mla_proj",
    )(h, cos, sin, *[a for a, _ in ops])


def _attn_kernel(q_ref, k_ref, v_ref, o_ref, vext_ref):
    S = q_ref.shape[2]
    tq = ATTN_Q_ROWS
    for hd in range(ATTN_HEADS_PER_STEP):
        vext_ref[hd, :, :V_HEAD_DIM] = v_ref[0, hd]
        vext_ref[hd, :, V_HEAD_DIM:] = jnp.ones((S, V_EXT_DIM - V_HEAD_DIM), _BF16)
    row = lax.broadcasted_iota(jnp.int32, (tq, tq), 0)
    col = lax.broadcasted_iota(jnp.int32, (tq, tq), 1)
    diag_mask = col <= row
    nt_dims = (((1,), (1,)), ((), ()))
    for j in range(S // tq):
        q0, kend = j * tq, (j + 1) * tq
        for hd in range(ATTN_HEADS_PER_STEP):
            q = q_ref[0, hd, q0:kend, :]
            s_diag = lax.dot_general(q, k_ref[0, hd, q0:kend, :], nt_dims,
                                     preferred_element_type=_F32)
            s_diag = jnp.where(diag_mask, s_diag, -jnp.inf)
            m = jnp.max(s_diag, axis=-1, keepdims=True)
            if j:
                s_past = lax.dot_general(q, k_ref[0, hd, :q0, :], nt_dims,
                                         preferred_element_type=_F32)
                m = jnp.maximum(m, jnp.max(s_past, axis=-1, keepdims=True))
                o = _dot(jnp.exp2(s_past - m).astype(_BF16), vext_ref[hd, :q0, :])
                o = o + _dot(jnp.exp2(s_diag - m).astype(_BF16), vext_ref[hd, q0:kend, :])
            else:
                o = _dot(jnp.exp2(s_diag - m).astype(_BF16), vext_ref[hd, q0:kend, :])
            o_ref[0, q0:kend, hd * V_HEAD_DIM:(hd + 1) * V_HEAD_DIM] = (
                o[:, :V_HEAD_DIM] / o[:, V_HEAD_DIM:]).astype(_BF16)


def _attention(q, k, v):
    B, H, S, _ = q.shape
    hps = ATTN_HEADS_PER_STEP
    streamed = hps * (2 * _nbytes((S, QK_PAD_DIM), _BF16) + 2 * _nbytes((S, V_HEAD_DIM), _BF16))
    scratch = 8 * _nbytes((ATTN_Q_ROWS, S), _F32) + hps * _nbytes((S, V_EXT_DIM), _BF16)
    return pl.pallas_call(
        _attn_kernel,
        grid=(B, H // hps),
        in_specs=[pl.BlockSpec((1, hps, S, QK_PAD_DIM), lambda b, h: (b, h, 0, 0)),
                  pl.BlockSpec((1, hps, S, QK_PAD_DIM), lambda b, h: (b, h, 0, 0)),
                  pl.BlockSpec((1, hps, S, V_HEAD_DIM), lambda b, h: (b, h, 0, 0))],
        out_specs=pl.BlockSpec((1, S, hps * V_HEAD_DIM), lambda b, h: (b, 0, h)),
        out_shape=jax.ShapeDtypeStruct((B, S, H * V_HEAD_DIM), _BF16),
        scratch_shapes=[pltpu.VMEM((hps, S, V_EXT_DIM), _BF16)],
        compiler_params=pltpu.CompilerParams(
            dimension_semantics=("parallel", "parallel"),
            vmem_limit_bytes=_vmem_limit(0, streamed, scratch)),
        name="mla_attention",
    )(q, k, v)


def _post_kernel(mix_ref, h_ref, p_ref, wmix_ref, gffn_ref, wup_ref, wdn_ref, gple_ref,
                 wgate_ref, wproj_ref, out_ref):
    h = h_ref[...] + _dot(mix_ref[...], wmix_ref[...])
    hn = (h * _rms_scale(h, D_MODEL) * gffn_ref[...]).astype(_BF16)
    for c in range(D_FF // FF_CHUNK):
        up = _dot(hn, wup_ref[:, c * FF_CHUNK:(c + 1) * FF_CHUNK])
        act = jnp.square(jnp.maximum(up, 0.0)).astype(_BF16)
        h = h + _dot(act, wdn_ref[c * FF_CHUNK:(c + 1) * FF_CHUNK, :])
    hn = (h * _rms_scale(h, D_MODEL) * gple_ref[...]).astype(_BF16)
    gate = jax.nn.sigmoid(_dot(hn, wgate_ref[...]))
    proj = _dot(p_ref[...].astype(_BF16), wproj_ref[...])
    out_ref[...] = h + gate * proj


def _post(mix, h, p, layer, ops):
    T, D = h.shape
    rows = POST_ROWS
    kmix = mix.shape[1]
    resident = _slab_bytes([a for a, _ in ops])
    streamed = (_nbytes((rows, kmix), _BF16) + 2 * _nbytes((rows, D), _F32)
                + _nbytes((rows, PLE_DIM), _F32))
    scratch = 2 * _nbytes((rows, FF_CHUNK), _F32) + 4 * _nbytes((rows, D), _F32)
    return pl.pallas_call(
        _post_kernel,
        grid=(T // rows,),
        in_specs=[pl.BlockSpec((rows, kmix), lambda i: (i, 0)),
                  pl.BlockSpec((rows, D), lambda i: (i, 0)),
                  pl.BlockSpec((None, rows, PLE_DIM), lambda i: (layer, i, 0))]
                 + [_layer_spec(a, l) for a, l in ops],
        out_specs=pl.BlockSpec((rows, D), lambda i: (i, 0)),
        out_shape=jax.ShapeDtypeStruct((T, D), _F32),
        compiler_params=pltpu.CompilerParams(
            dimension_semantics=("parallel",),
            vmem_limit_bytes=_vmem_limit(resident, streamed, scratch)),
        name="post_ffn_ple",
    )(mix, h, p, *[a for a, _ in ops])


_GELU_C = 0.7978845608028654
_GELU_A = 0.044715


def _gelu_tanh(x):
    half_x = 0.5 * x
    inner = x * (_GELU_C + (_GELU_C * _GELU_A) * (x * x))
    return half_x + half_x * jnp.tanh(inner)


def _gmlp_rows(row0, ws_masked, h_ref, gmix_ref, win_ref, lng_ref, lnb_ref, bst_ref, y_ref):
    h = h_ref[row0:row0 + GMLP_SUB_ROWS, :]
    hn = (h * _rms_scale(h, D_MODEL) * gmix_ref[...]).astype(_BF16)
    v = _gelu_tanh(_dot(hn, win_ref[:, GMLP_HALF:]))
    mu = jnp.mean(v, axis=-1, keepdims=True)
    vc = v - mu
    var = jnp.mean(vc * vc, axis=-1, keepdims=True)
    vn = (vc * lax.rsqrt(var + NORM_EPS) * lng_ref[...] + lnb_ref[...]).astype(_BF16)
    u = _gelu_tanh(_dot(hn, win_ref[:, :GMLP_HALF]))
    for g in range(GMLP_GROUPS):
        bias = bst_ref[:, g:g + 1]
        c0 = g * GMLP_GROUP_DIM
        for n in range(GMLP_SUB_ROWS // GMLP_CHUNK):
            r0 = n * GMLP_CHUNK
            sv = _dot(ws_masked[g], vn[r0:r0 + GMLP_CHUNK, c0:c0 + GMLP_GROUP_DIM]) + bias
            y_ref[row0 + r0:row0 + r0 + GMLP_CHUNK, c0:c0 + GMLP_GROUP_DIM] = (
                u[r0:r0 + GMLP_CHUNK, c0:c0 + GMLP_GROUP_DIM] * sv).astype(_BF16)


def _gmlp_kernel(h_ref, gmix_ref, win_ref, lng_ref, lnb_ref, ws_ref, bst_ref, y_ref):
    t_idx = lax.broadcasted_iota(jnp.int32, (GMLP_CHUNK, GMLP_CHUNK), 0)
    s_idx = lax.broadcasted_iota(jnp.int32, (GMLP_CHUNK, GMLP_CHUNK), 1)
    causal = s_idx <= t_idx
    ws_masked = [jnp.where(causal, ws_ref[g], 0.0).astype(_BF16) for g in range(GMLP_GROUPS)]
    for row0 in range(0, GMLP_ROWS, GMLP_SUB_ROWS):
        _gmlp_rows(row0, ws_masked, h_ref, gmix_ref, win_ref, lng_ref, lnb_ref, bst_ref, y_ref)


def _gmlp_front(h, ops):
    T, D = h.shape
    rows = GMLP_ROWS
    resident = _slab_bytes([a for a, _ in ops])
    streamed = _nbytes((rows, D), _F32) + _nbytes((rows, GMLP_HALF), _BF16)
    scratch = 6 * _nbytes((rows, GMLP_HALF), _F32)
    return pl.pallas_call(
        _gmlp_kernel,
        grid=(T // rows,),
        in_specs=[pl.BlockSpec((rows, D), lambda i: (i, 0))] + [_layer_spec(a, l) for a, l in ops],
        out_specs=pl.BlockSpec((rows, GMLP_HALF), lambda i: (i, 0)),
        out_shape=jax.ShapeDtypeStruct((T, GMLP_HALF), _BF16),
        compiler_params=pltpu.CompilerParams(
            dimension_semantics=("parallel",),
            vmem_limit_bytes=_vmem_limit(resident, streamed, scratch)),
        name="gmlp_front",
    )(h, *[a for a, _ in ops])


def _rope_pad(w):
    z = jnp.zeros(w.shape[:-1] + (ROPE_HALF,), w.dtype)
    return jnp.concatenate([w[..., :ROPE_HALF], z, w[..., ROPE_HALF:], z], axis=-1)


def _rows(g):
    return g.reshape(g.shape[0], 1, g.shape[-1]).astype(_F32)


def _mla_params(w_down, w_uq, w_ukv, q_nope_g, q_rope_g, k_nope_g, k_rope_g):
    H = MLA_HEADS
    L = w_down.shape[0]
    n_lat = Q_LORA_RANK + KV_LORA_RANK
    wdown = jnp.concatenate([w_down[..., :n_lat], _rope_pad(w_down[..., n_lat:])], axis=-1)
    uq = w_uq.reshape(L, Q_LORA_RANK, H, QK_NOPE_DIM + QK_ROPE_DIM)
    wuq = jnp.concatenate([uq[..., :QK_NOPE_DIM], _rope_pad(uq[..., QK_NOPE_DIM:])], axis=-1)
    wuq = wuq.reshape(L, Q_LORA_RANK, H * QK_PAD_DIM)
    ukv = w_ukv.reshape(L, KV_LORA_RANK, H, QK_NOPE_DIM + V_HEAD_DIM)
    wukv = jnp.concatenate([ukv[..., :QK_NOPE_DIM].reshape(L, KV_LORA_RANK, H * QK_NOPE_DIM),
                            ukv[..., QK_NOPE_DIM:].reshape(L, KV_LORA_RANK, H * V_HEAD_DIM)], axis=-1)
    q_scale = (QK_NOPE_DIM + QK_ROPE_DIM) ** -0.5 * math.log2(math.e)
    gq = _rows(jnp.concatenate([q_nope_g, _rope_pad(q_rope_g)], axis=-1)) * q_scale
    gkn = _rows(jnp.concatenate([k_nope_g, k_nope_g], axis=-1))
    return (wdown.astype(_BF16), wuq.astype(_BF16), wukv.astype(_BF16), gq, gkn,
            _rows(_rope_pad(k_rope_g)))


def _group_mean_matrix(group_sizes, counts):
    blocks = []
    width = sum(group_sizes)
    off = 0
    for size, count in zip(group_sizes, counts):
        col = jnp.zeros((size, width), _F32).at[:, off:off + size].set(1.0 / count)
        blocks.append(col)
        off += size
    return jnp.concatenate(blocks, axis=0).astype(_BF16)[None]


def _rope_consts():
    inv_freq = ROPE_BASE ** (-(jnp.arange(0, QK_ROPE_DIM, 2, dtype=_F32) / QK_ROPE_DIM))
    invf = _rope_pad(jnp.concatenate([inv_freq, inv_freq])).reshape(1, ROPE_PAD)
    ones = jnp.ones((ROPE_HALF,), _F32)
    sgn = _rope_pad(jnp.concatenate([-ones, ones])).reshape(1, ROPE_PAD)
    return invf, sgn


def _stat_consts():
    lanes = V7X_LANES
    mh = jnp.full((1, D_MODEL, lanes), 1.0 / D_MODEL, _BF16)
    mcq = jnp.ones((1, Q_LORA_RANK, lanes), _BF16)
    mckv = jnp.full((1, KV_LORA_RANK, lanes), 1.0 / KV_LORA_RANK, _BF16)
    mkr = jnp.full((1, ROPE_PAD, lanes), 1.0 / QK_ROPE_DIM, _BF16)
    mq = _group_mean_matrix((QK_NOPE_DIM, ROPE_PAD), (QK_NOPE_DIM, QK_ROPE_DIM))
    mk = _group_mean_matrix((QK_NOPE_DIM, QK_NOPE_DIM), (QK_NOPE_DIM, QK_NOPE_DIM))
    return [(c, 0) for c in (mh, mcq, mckv, mkr, mq, mk)]


def kernel(x, p, positions, norm_mix, norm_ffn, norm_ple, mla_w_down, mla_q_lora_g, mla_kv_lora_g,
           mla_w_uq, mla_w_ukv, mla_q_nope_g, mla_q_rope_g, mla_k_nope_g, mla_k_rope_g, mla_w_out,
           gmlp_w_in, gmlp_ln_g, gmlp_ln_b, gmlp_w_s, gmlp_b_s, gmlp_w_out, ffn_w_up, ffn_w_down,
           ple_w_gate, ple_w_proj):
    B, S, D = x.shape
    T = B * S
    pos = positions.reshape(B, S, 1)
    p_flat = p.reshape(DEPTH, T, PLE_DIM)
    g_mix, g_ffn, g_ple = _rows(norm_mix), _rows(norm_ffn), _rows(norm_ple)
    wdown, wuq, wukv, gq, gkn, gkr = _mla_params(mla_w_down, mla_w_uq, mla_w_ukv, mla_q_nope_g,
                                                 mla_q_rope_g, mla_k_nope_g, mla_k_rope_g)
    gql, gkvl = _rows(mla_q_lora_g), _rows(mla_kv_lora_g)
    mla_wout = mla_w_out.astype(_BF16)
    gmlp_win, gmlp_wout = gmlp_w_in.astype(_BF16), gmlp_w_out.astype(_BF16)
    ln_g, ln_b = _rows(gmlp_ln_g), _rows(gmlp_ln_b)
    bst = jnp.swapaxes(gmlp_b_s, 1, 2)
    w_up, w_dn = ffn_w_up.astype(_BF16), ffn_w_down.astype(_BF16)
    w_gate, w_proj = ple_w_gate.astype(_BF16), ple_w_proj.astype(_BF16)
    shared = _stat_consts()
    cos, sin = _rope_tables(pos, *_rope_consts())

    h = x.reshape(T, D)
    for i in range(DEPTH):
        j = i // 2
        if i % 2 == 0:
            layer_ops = [(g_mix, i), (wdown, j), (gql, j), (gkvl, j), (wuq, j), (wukv, j), (gq, j),
                         (gkn, j), (gkr, j)]
            q, k, v = _mla_proj(h.reshape(B, S, D), cos, sin, layer_ops, shared)
            mix = _attention(q, k, v).reshape(T, MLA_HEADS * V_HEAD_DIM)
            wmix = (mla_wout, j)
        else:
            mix = _gmlp_front(h, [(g_mix, i), (gmlp_win, j), (ln_g, j), (ln_b, j), (gmlp_w_s, j),
                                  (bst, j)])
            wmix = (gmlp_wout, j)
        h = _post(mix, h, p_flat, i, [wmix, (g_ffn, i), (w_up, i), (w_dn, i), (g_ple, i),
                                      (w_gate, i), (w_proj, i)])
    return h.reshape(B, S, D)
```

```python
import math

import jax
import jax.numpy as jnp
from jax import lax
from jax.experimental import pallas as pl
from jax.experimental.pallas import tpu as pltpu

D_MODEL = 1024
DEPTH = 4
MLA_HEADS = 8
QK_NOPE_DIM = 128
QK_ROPE_DIM = 64
V_HEAD_DIM = 128
Q_LORA_RANK = 384
KV_LORA_RANK = 256
ROPE_BASE = 10000.0
GMLP_CHUNK = 128
GMLP_HALF = 2 * D_MODEL
GMLP_GROUPS = 8
GMLP_GROUP_DIM = GMLP_HALF // GMLP_GROUPS
D_FF = 4 * D_MODEL
PLE_DIM = 256
NORM_EPS = 1e-6

V7X_LANES = 128
V7X_VMEM_REQUEST_CAP = 56 << 20

ROPE_HALF = QK_ROPE_DIM // 2
ROPE_PAD = V7X_LANES
QK_PAD_DIM = QK_NOPE_DIM + ROPE_PAD
LAT_PAD_DIM = Q_LORA_RANK + KV_LORA_RANK + ROPE_PAD
V_EXT_DIM = 2 * V_HEAD_DIM

PROJ_ROWS = 512
PROJ_SUB_ROWS = 256
ATTN_Q_ROWS = 256
ATTN_HEADS_PER_STEP = 4
POST_ROWS = 512
FF_CHUNK = 1024
GMLP_ROWS = 512
GMLP_SUB_ROWS = 256
_BF16 = jnp.bfloat16
_F32 = jnp.float32


def _vmem_limit(resident_bytes, streamed_bytes, scratch_bytes):
    need = resident_bytes + 2 * streamed_bytes + scratch_bytes
    return min(V7X_VMEM_REQUEST_CAP, need)


def _nbytes(shape, dtype):
    return math.prod(shape) * jnp.dtype(dtype).itemsize


def _layer_spec(arr, layer):
    idx = (layer,) + (0,) * (arr.ndim - 1)
    return pl.BlockSpec((None,) + arr.shape[1:], lambda *_: idx, pipeline_mode=pl.Buffered(1))


def _slab_bytes(arrs):
    return sum(_nbytes(a.shape[1:], a.dtype) for a in arrs)


def _dot(a, b):
    return jnp.dot(a, b, preferred_element_type=_F32)


def _rms_scale(x, n):
    ss = jnp.sum(x * x, axis=-1, keepdims=True)
    return lax.rsqrt(ss / n + NORM_EPS)


def _mean_sq(x, m):
    return _dot((x * x).astype(_BF16), m)


def _lane_tile(r, width):
    return jnp.concatenate([r] * (width // r.shape[1]), axis=1)


def _rope_table_kernel(pos_ref, invf_ref, sgn_ref, cos_ref, sin_ref):
    ang = pos_ref[0].astype(_F32) * invf_ref[...]
    cos_ref[0] = jnp.cos(ang)
    sin_ref[0] = jnp.sin(ang) * sgn_ref[...]


def _rope_tables(pos, invf, sgn):
    B, S, _ = pos.shape
    rows = PROJ_ROWS
    out = jax.ShapeDtypeStruct((B, S, ROPE_PAD), _F32)
    table_spec = pl.BlockSpec((1, rows, ROPE_PAD), lambda b, i: (b, i, 0))
    const_spec = pl.BlockSpec((1, ROPE_PAD), lambda b, i: (0, 0))
    return pl.pallas_call(
        _rope_table_kernel,
        grid=(B, S // rows),
        in_specs=[pl.BlockSpec((1, rows, 1), lambda b, i: (b, i, 0)), const_spec, const_spec],
        out_specs=[table_spec, table_spec],
        out_shape=[out, out],
        compiler_params=pltpu.CompilerParams(dimension_semantics=("parallel", "parallel")),
        name="rope_tables",
    )(pos, invf, sgn)


def _mla_proj_rows(r0, h_ref, cos_ref, sin_ref, gmix_ref, wdown_ref, gql_ref, gkvl_ref, wuq_ref,
                   wukv_ref, gq_ref, gkn_ref, gkr_ref, mh_ref, mcq_ref, mckv_ref, mkr_ref, mq_ref,
                   mk_ref, q_ref, k_ref, v_ref):
    rs = slice(r0, r0 + PROJ_SUB_ROWS)
    h = h_ref[0, rs, :]
    r_h = lax.rsqrt(_mean_sq(h, mh_ref[...]) + NORM_EPS)
    hn = (h * _lane_tile(r_h, D_MODEL) * gmix_ref[...]).astype(_BF16)
    lat = _dot(hn, wdown_ref[...])
    c_q = lat[:, :Q_LORA_RANK]
    c_kv = lat[:, Q_LORA_RANK:Q_LORA_RANK + KV_LORA_RANK]
    k_rope = lat[:, Q_LORA_RANK + KV_LORA_RANK:]
    r_cq = lax.rsqrt(_mean_sq(c_q, mcq_ref[...]) * (1.0 / Q_LORA_RANK) + NORM_EPS)
    c_q = (c_q * _lane_tile(r_cq, Q_LORA_RANK) * gql_ref[...]).astype(_BF16)
    r_ckv = lax.rsqrt(_mean_sq(c_kv, mckv_ref[...]) + NORM_EPS)
    c_kv = (c_kv * _lane_tile(r_ckv, KV_LORA_RANK) * gkvl_ref[...]).astype(_BF16)

    cos = cos_ref[0, rs, :]
    sin_signed = sin_ref[0, rs, :]

    def rope(x):
        return x * cos + pltpu.roll(x, ROPE_PAD // 2, 1) * sin_signed

    r_kr = lax.rsqrt(_mean_sq(k_rope, mkr_ref[...]) + NORM_EPS)
    k_rope = rope(k_rope * r_kr * gkr_ref[...]).astype(_BF16)

    q = _dot(c_q, wuq_ref[:, :MLA_HEADS * QK_PAD_DIM])
    kv = _dot(c_kv, wukv_ref[:, :MLA_HEADS * (QK_NOPE_DIM + V_HEAD_DIM)])
    gq = gq_ref[...]
    gkn = gkn_ref[...]
    for hd in range(MLA_HEADS):
        xq = q[:, hd * QK_PAD_DIM:(hd + 1) * QK_PAD_DIM]
        yq = xq * lax.rsqrt(_mean_sq(xq, mq_ref[...]) + NORM_EPS) * gq
        q_ref[0, hd, rs, :QK_NOPE_DIM] = yq[:, :QK_NOPE_DIM].astype(_BF16)
        q_ref[0, hd, rs, QK_NOPE_DIM:] = rope(yq[:, QK_NOPE_DIM:]).astype(_BF16)
        k_ref[0, hd, rs, QK_NOPE_DIM:] = k_rope
        v0 = MLA_HEADS * QK_NOPE_DIM + hd * V_HEAD_DIM
        v_ref[0, hd, rs, :] = kv[:, v0:v0 + V_HEAD_DIM].astype(_BF16)
    for pair in range(MLA_HEADS // 2):
        xk = kv[:, 2 * pair * QK_NOPE_DIM:(2 * pair + 2) * QK_NOPE_DIM]
        yk = (xk * lax.rsqrt(_mean_sq(xk, mk_ref[...]) + NORM_EPS) * gkn).astype(_BF16)
        k_ref[0, 2 * pair, rs, :QK_NOPE_DIM] = yk[:, :QK_NOPE_DIM]
        k_ref[0, 2 * pair + 1, rs, :QK_NOPE_DIM] = yk[:, QK_NOPE_DIM:]


def _mla_proj_kernel(*refs):
    for r0 in range(0, PROJ_ROWS, PROJ_SUB_ROWS):
        _mla_proj_rows(r0, *refs)


def _mla_proj(h, cos, sin, layer_ops, shared_ops):
    B, S, D = h.shape
    rows = PROJ_ROWS
    H = MLA_HEADS
    ops = layer_ops + shared_ops
    resident = _slab_bytes([a for a, _ in ops])
    streamed = (_nbytes((rows, D), _F32) + 2 * _nbytes((rows, ROPE_PAD), _F32)
                + 2 * _nbytes((H, rows, QK_PAD_DIM), _BF16) + _nbytes((H, rows, V_HEAD_DIM), _BF16))
    scratch = 4 * _nbytes((rows, H * QK_PAD_DIM), _F32)
    return pl.pallas_call(
        _mla_proj_kernel,
        grid=(B, S // rows),
        in_specs=[pl.BlockSpec((1, rows, D), lambda b, i: (b, i, 0)),
                  pl.BlockSpec((1, rows, ROPE_PAD), lambda b, i: (b, i, 0)),
                  pl.BlockSpec((1, rows, ROPE_PAD), lambda b, i: (b, i, 0))]
                 + [_layer_spec(a, l) for a, l in ops],
        out_specs=[pl.BlockSpec((1, H, rows, QK_PAD_DIM), lambda b, i: (b, 0, i, 0)),
                   pl.BlockSpec((1, H, rows, QK_PAD_DIM), lambda b, i: (b, 0, i, 0)),
                   pl.BlockSpec((1, H, rows, V_HEAD_DIM), lambda b, i: (b, 0, i, 0))],
        out_shape=[jax.ShapeDtypeStruct((B, H, S, QK_PAD_DIM), _BF16),
                   jax.ShapeDtypeStruct((B, H, S, QK_PAD_DIM), _BF16),
                   jax.ShapeDtypeStruct((B, H, S, V_HEAD_DIM), _BF16)],
        compiler_params=pltpu.CompilerParams(
            dimension_semantics=("parallel", "parallel"),
            vmem_limit_bytes=_vmem_limit(resident, streamed, scratch)),
        name="mla_proj",
    )(h, cos, sin, *[a for a, _ in ops])


def _attn_kernel(q_ref, k_ref, v_ref, o_ref, vext_ref):
    S = q_ref.shape[2]
    tq = ATTN_Q_ROWS
    for hd in range(ATTN_HEADS_PER_STEP):
        vext_ref[hd, :, :V_HEAD_DIM] = v_ref[0, hd]
        vext_ref[hd, :, V_HEAD_DIM:] = jnp.ones((S, V_EXT_DIM - V_HEAD_DIM), _BF16)
    row = lax.broadcasted_iota(jnp.int32, (tq, tq), 0)
    col = lax.broadcasted_iota(jnp.int32, (tq, tq), 1)
    diag_mask = col <= row
    nt_dims = (((1,), (1,)), ((), ()))
    for j in range(S // tq):
        q0, kend = j * tq, (j + 1) * tq
        for hd in range(ATTN_HEADS_PER_STEP):
            q = q_ref[0, hd, q0:kend, :]
            s_diag = lax.dot_general(q, k_ref[0, hd, q0:kend, :], nt_dims,
                                     preferred_element_type=_F32)
            s_diag = jnp.where(diag_mask, s_diag, -jnp.inf)
            m = jnp.max(s_diag, axis=-1, keepdims=True)
            if j:
                s_past = lax.dot_general(q, k_ref[0, hd, :q0, :], nt_dims,
                                         preferred_element_type=_F32)
                m = jnp.maximum(m, jnp.max(s_past, axis=-1, keepdims=True))
                o = _dot(jnp.exp2(s_past - m).astype(_BF16), vext_ref[hd, :q0, :])
                o = o + _dot(jnp.exp2(s_diag - m).astype(_BF16), vext_ref[hd, q0:kend, :])
            else:
                o = _dot(jnp.exp2(s_diag - m).astype(_BF16), vext_ref[hd, q0:kend, :])
            o_ref[0, q0:kend, hd * V_HEAD_DIM:(hd + 1) * V_HEAD_DIM] = (
                o[:, :V_HEAD_DIM] / o[:, V_HEAD_DIM:]).astype(_BF16)


def _attention(q, k, v):
    B, H, S, _ = q.shape
    hps = ATTN_HEADS_PER_STEP
    streamed = hps * (2 * _nbytes((S, QK_PAD_DIM), _BF16) + 2 * _nbytes((S, V_HEAD_DIM), _BF16))
    scratch = 8 * _nbytes((ATTN_Q_ROWS, S), _F32) + hps * _nbytes((S, V_EXT_DIM), _BF16)
    return pl.pallas_call(
        _attn_kernel,
        grid=(B, H // hps),
        in_specs=[pl.BlockSpec((1, hps, S, QK_PAD_DIM), lambda b, h: (b, h, 0, 0)),
                  pl.BlockSpec((1, hps, S, QK_PAD_DIM), lambda b, h: (b, h, 0, 0)),
                  pl.BlockSpec((1, hps, S, V_HEAD_DIM), lambda b, h: (b, h, 0, 0))],
        out_specs=pl.BlockSpec((1, S, hps * V_HEAD_DIM), lambda b, h: (b, 0, h)),
        out_shape=jax.ShapeDtypeStruct((B, S, H * V_HEAD_DIM), _BF16),
        scratch_shapes=[pltpu.VMEM((hps, S, V_EXT_DIM), _BF16)],
        compiler_params=pltpu.CompilerParams(
            dimension_semantics=("parallel", "parallel"),
            vmem_limit_bytes=_vmem_limit(0, streamed, scratch)),
        name="mla_attention",
    )(q, k, v)


def _post_kernel(mix_ref, h_ref, p_ref, wmix_ref, gffn_ref, wup_ref, wdn_ref, gple_ref,
                 wgate_ref, wproj_ref, out_ref):
    h = h_ref[...] + _dot(mix_ref[...], wmix_ref[:, :D_MODEL])
    hn = (h * _rms_scale(h, D_MODEL) * gffn_ref[...]).astype(_BF16)
    for c in range(D_FF // FF_CHUNK):
        up = _dot(hn, wup_ref[:, c * FF_CHUNK:(c + 1) * FF_CHUNK])
        act = jnp.square(jnp.maximum(up, 0.0)).astype(_BF16)
        h = h + _dot(act, wdn_ref[c * FF_CHUNK:(c + 1) * FF_CHUNK, :D_MODEL])
    hn = (h * _rms_scale(h, D_MODEL) * gple_ref[...]).astype(_BF16)
    gate = jax.nn.sigmoid(_dot(hn, wgate_ref[:, :D_MODEL]))
    proj = _dot(p_ref[...].astype(_BF16), wproj_ref[:, :D_MODEL])
    out_ref[...] = h + gate * proj


def _post(mix, h, p, layer, ops):
    T, D = h.shape
    rows = POST_ROWS
    kmix = mix.shape[1]
    resident = _slab_bytes([a for a, _ in ops])
    streamed = (_nbytes((rows, kmix), _BF16) + 2 * _nbytes((rows, D), _F32)
                + _nbytes((rows, PLE_DIM), _F32))
    scratch = 2 * _nbytes((rows, FF_CHUNK), _F32) + 4 * _nbytes((rows, D), _F32)
    return pl.pallas_call(
        _post_kernel,
        grid=(T // rows,),
        in_specs=[pl.BlockSpec((rows, kmix), lambda i: (i, 0)),
                  pl.BlockSpec((rows, D), lambda i: (i, 0)),
                  pl.BlockSpec((None, rows, PLE_DIM), lambda i: (layer, i, 0))]
                 + [_layer_spec(a, l) for a, l in ops],
        out_specs=pl.BlockSpec((rows, D), lambda i: (i, 0)),
        out_shape=jax.ShapeDtypeStruct((T, D), _F32),
        compiler_params=pltpu.CompilerParams(
            dimension_semantics=("parallel",),
            vmem_limit_bytes=_vmem_limit(resident, streamed, scratch)),
        name="post_ffn_ple",
    )(mix, h, p, *[a for a, _ in ops])


_GELU_C = 0.7978845608028654
_GELU_A = 0.044715


def _gelu_tanh(x):
    half_x = 0.5 * x
    inner = x * (_GELU_C + (_GELU_C * _GELU_A) * (x * x))
    return half_x + half_x * jnp.tanh(inner)


def _gmlp_rows(row0, ws_masked, h_ref, gmix_ref, win_ref, lng_ref, lnb_ref, bst_ref, y_ref):
    h = h_ref[row0:row0 + GMLP_SUB_ROWS, :]
    hn = (h * _rms_scale(h, D_MODEL) * gmix_ref[...]).astype(_BF16)
    v = _gelu_tanh(_dot(hn, win_ref[:, GMLP_HALF:2 * GMLP_HALF]))
    mu = jnp.mean(v, axis=-1, keepdims=True)
    vc = v - mu
    var = jnp.mean(vc * vc, axis=-1, keepdims=True)
    vn = (vc * lax.rsqrt(var + NORM_EPS) * lng_ref[...] + lnb_ref[...]).astype(_BF16)
    u = _gelu_tanh(_dot(hn, win_ref[:, :GMLP_HALF]))
    for g in range(GMLP_GROUPS):
        bias = bst_ref[:, g:g + 1]
        c0 = g * GMLP_GROUP_DIM
        for n in range(GMLP_SUB_ROWS // GMLP_CHUNK):
            r0 = n * GMLP_CHUNK
            sv = _dot(ws_masked[g], vn[r0:r0 + GMLP_CHUNK, c0:c0 + GMLP_GROUP_DIM]) + bias
            y_ref[row0 + r0:row0 + r0 + GMLP_CHUNK, c0:c0 + GMLP_GROUP_DIM] = (
                u[r0:r0 + GMLP_CHUNK, c0:c0 + GMLP_GROUP_DIM] * sv).astype(_BF16)


def _gmlp_kernel(h_ref, gmix_ref, win_ref, lng_ref, lnb_ref, ws_ref, bst_ref, y_ref):
    t_idx = lax.broadcasted_iota(jnp.int32, (GMLP_CHUNK, GMLP_CHUNK), 0)
    s_idx = lax.broadcasted_iota(jnp.int32, (GMLP_CHUNK, GMLP_CHUNK), 1)
    causal = s_idx <= t_idx
    ws_masked = [jnp.where(causal, ws_ref[g], 0.0).astype(_BF16) for g in range(GMLP_GROUPS)]
    for row0 in range(0, GMLP_ROWS, GMLP_SUB_ROWS):
        _gmlp_rows(row0, ws_masked, h_ref, gmix_ref, win_ref, lng_ref, lnb_ref, bst_ref, y_ref)


def _gmlp_front(h, ops):
    T, D = h.shape
    rows = GMLP_ROWS
    resident = _slab_bytes([a for a, _ in ops])
    streamed = _nbytes((rows, D), _F32) + _nbytes((rows, GMLP_HALF), _BF16)
    scratch = 6 * _nbytes((rows, GMLP_HALF), _F32)
    return pl.pallas_call(
        _gmlp_kernel,
        grid=(T // rows,),
        in_specs=[pl.BlockSpec((rows, D), lambda i: (i, 0))] + [_layer_spec(a, l) for a, l in ops],
        out_specs=pl.BlockSpec((rows, GMLP_HALF), lambda i: (i, 0)),
        out_shape=jax.ShapeDtypeStruct((T, GMLP_HALF), _BF16),
        compiler_params=pltpu.CompilerParams(
            dimension_semantics=("parallel",),
            vmem_limit_bytes=_vmem_limit(resident, streamed, scratch)),
        name="gmlp_front",
    )(h, *[a for a, _ in ops])


def _rope_pad(w):
    z = jnp.zeros(w.shape[:-1] + (ROPE_HALF,), w.dtype)
    return jnp.concatenate([w[..., :ROPE_HALF], z, w[..., ROPE_HALF:], z], axis=-1)


def _skew_pitch(w):
    pad = [(0, 0)] * (w.ndim - 1) + [(0, V7X_LANES)]
    return jnp.pad(w.astype(_BF16), pad)


def _rows(g):
    return g.reshape(g.shape[0], 1, g.shape[-1]).astype(_F32)


def _mla_params(w_down, w_uq, w_ukv, q_nope_g, q_rope_g, k_nope_g, k_rope_g):
    H = MLA_HEADS
    L = w_down.shape[0]
    n_lat = Q_LORA_RANK + KV_LORA_RANK
    wdown = jnp.concatenate([w_down[..., :n_lat], _rope_pad(w_down[..., n_lat:])], axis=-1)
    uq = w_uq.reshape(L, Q_LORA_RANK, H, QK_NOPE_DIM + QK_ROPE_DIM)
    wuq = jnp.concatenate([uq[..., :QK_NOPE_DIM], _rope_pad(uq[..., QK_NOPE_DIM:])], axis=-1)
    wuq = wuq.reshape(L, Q_LORA_RANK, H * QK_PAD_DIM)
    ukv = w_ukv.reshape(L, KV_LORA_RANK, H, QK_NOPE_DIM + V_HEAD_DIM)
    wukv = jnp.concatenate([ukv[..., :QK_NOPE_DIM].reshape(L, KV_LORA_RANK, H * QK_NOPE_DIM),
                            ukv[..., QK_NOPE_DIM:].reshape(L, KV_LORA_RANK, H * V_HEAD_DIM)], axis=-1)
    q_scale = (QK_NOPE_DIM + QK_ROPE_DIM) ** -0.5 * math.log2(math.e)
    gq = _rows(jnp.concatenate([q_nope_g, _rope_pad(q_rope_g)], axis=-1)) * q_scale
    gkn = _rows(jnp.concatenate([k_nope_g, k_nope_g], axis=-1))
    return (wdown.astype(_BF16), _skew_pitch(wuq), _skew_pitch(wukv), gq, gkn,
            _rows(_rope_pad(k_rope_g)))


def _group_mean_matrix(group_sizes, counts):
    blocks = []
    width = sum(group_sizes)
    off = 0
    for size, count in zip(group_sizes, counts):
        col = jnp.zeros((size, width), _F32).at[:, off:off + size].set(1.0 / count)
        blocks.append(col)
        off += size
    return jnp.concatenate(blocks, axis=0).astype(_BF16)[None]


def _rope_consts():
    inv_freq = ROPE_BASE ** (-(jnp.arange(0, QK_ROPE_DIM, 2, dtype=_F32) / QK_ROPE_DIM))
    invf = _rope_pad(jnp.concatenate([inv_freq, inv_freq])).reshape(1, ROPE_PAD)
    ones = jnp.ones((ROPE_HALF,), _F32)
    sgn = _rope_pad(jnp.concatenate([-ones, ones])).reshape(1, ROPE_PAD)
    return invf, sgn


def _stat_consts():
    lanes = V7X_LANES
    mh = jnp.full((1, D_MODEL, lanes), 1.0 / D_MODEL, _BF16)
    mcq = jnp.ones((1, Q_LORA_RANK, lanes), _BF16)
    mckv = jnp.full((1, KV_LORA_RANK, lanes), 1.0 / KV_LORA_RANK, _BF16)
    mkr = jnp.full((1, ROPE_PAD, lanes), 1.0 / QK_ROPE_DIM, _BF16)
    mq = _group_mean_matrix((QK_NOPE_DIM, ROPE_PAD), (QK_NOPE_DIM, QK_ROPE_DIM))
    mk = _group_mean_matrix((QK_NOPE_DIM, QK_NOPE_DIM), (QK_NOPE_DIM, QK_NOPE_DIM))
    return [(c, 0) for c in (mh, mcq, mckv, mkr, mq, mk)]


def kernel(x, p, positions, norm_mix, norm_ffn, norm_ple, mla_w_down, mla_q_lora_g, mla_kv_lora_g,
           mla_w_uq, mla_w_ukv, mla_q_nope_g, mla_q_rope_g, mla_k_nope_g, mla_k_rope_g, mla_w_out,
           gmlp_w_in, gmlp_ln_g, gmlp_ln_b, gmlp_w_s, gmlp_b_s, gmlp_w_out, ffn_w_up, ffn_w_down,
           ple_w_gate, ple_w_proj):
    B, S, D = x.shape
    T = B * S
    pos = positions.reshape(B, S, 1)
    p_flat = p.reshape(DEPTH, T, PLE_DIM)
    g_mix, g_ffn, g_ple = _rows(norm_mix), _rows(norm_ffn), _rows(norm_ple)
    wdown, wuq, wukv, gq, gkn, gkr = _mla_params(mla_w_down, mla_w_uq, mla_w_ukv, mla_q_nope_g,
                                                 mla_q_rope_g, mla_k_nope_g, mla_k_rope_g)
    gql, gkvl = _rows(mla_q_lora_g), _rows(mla_kv_lora_g)
    mla_wout = _skew_pitch(mla_w_out)
    gmlp_win, gmlp_wout = _skew_pitch(gmlp_w_in), _skew_pitch(gmlp_w_out)
    ln_g, ln_b = _rows(gmlp_ln_g), _rows(gmlp_ln_b)
    bst = jnp.swapaxes(gmlp_b_s, 1, 2)
    w_up, w_dn = _skew_pitch(ffn_w_up), _skew_pitch(ffn_w_down)
    w_gate, w_proj = _skew_pitch(ple_w_gate), _skew_pitch(ple_w_proj)
    shared = _stat_consts()
    cos, sin = _rope_tables(pos, *_rope_consts())

    h = x.reshape(T, D)
    for i in range(DEPTH):
        j = i // 2
        if i % 2 == 0:
            layer_ops = [(g_mix, i), (wdown, j), (gql, j), (gkvl, j), (wuq, j), (wukv, j), (gq, j),
                         (gkn, j), (gkr, j)]
            q, k, v = _mla_proj(h.reshape(B, S, D), cos, sin, layer_ops, shared)
            mix = _attention(q, k, v).reshape(T, MLA_HEADS * V_HEAD_DIM)
            wmix = (mla_wout, j)
        else:
            mix = _gmlp_front(h, [(g_mix, i), (gmlp_win, j), (ln_g, j), (ln_b, j), (gmlp_w_s, j),
                                  (bst, j)])
            wmix = (gmlp_wout, j)
        h = _post(mix, h, p_flat, i, [wmix, (g_ffn, i), (w_up, i), (w_dn, i), (g_ple, i),
                                      (w_gate, i), (w_proj, i)])
    return h.reshape(B, S, D)
```

```python
import math

import jax
import jax.numpy as jnp
from jax import lax
from jax.experimental import pallas as pl
from jax.experimental.pallas import tpu as pltpu

D_MODEL = 1024
DEPTH = 4
MLA_HEADS = 8
QK_NOPE_DIM = 128
QK_ROPE_DIM = 64
V_HEAD_DIM = 128
Q_LORA_RANK = 384
KV_LORA_RANK = 256
ROPE_BASE = 10000.0
GMLP_CHUNK = 128
GMLP_HALF = 2 * D_MODEL
GMLP_GROUPS = 8
GMLP_GROUP_DIM = GMLP_HALF // GMLP_GROUPS
D_FF = 4 * D_MODEL
PLE_DIM = 256
NORM_EPS = 1e-6

V7X_LANES = 128
V7X_VMEM_REQUEST_CAP = 56 << 20

ROPE_HALF = QK_ROPE_DIM // 2
ROPE_PAD = V7X_LANES
ROPE_PACK = V7X_LANES // ROPE_HALF
QK_PAD_DIM = QK_NOPE_DIM + ROPE_PAD
LAT_PAD_DIM = Q_LORA_RANK + KV_LORA_RANK + ROPE_PAD
V_EXT_DIM = 2 * V_HEAD_DIM

PROJ_ROWS = 512
PROJ_SUB_ROWS = 256
ROPE_TABLE_ROWS = 1024
ATTN_Q_ROWS = 256
ATTN_SCORE_LOOKAHEAD = 1
ATTN_HEADS_PER_STEP = 4
POST_ROWS = 512
FF_CHUNK = 1024
GMLP_ROWS = 512
GMLP_SUB_ROWS = 256

_BF16 = jnp.bfloat16
_F32 = jnp.float32


def _vmem_limit(resident_bytes, streamed_bytes, scratch_bytes):
    need = resident_bytes + 2 * streamed_bytes + scratch_bytes
    return min(V7X_VMEM_REQUEST_CAP, need)


def _nbytes(shape, dtype):
    return math.prod(shape) * jnp.dtype(dtype).itemsize


def _layer_spec(arr, layer):
    idx = (layer,) + (0,) * (arr.ndim - 1)
    return pl.BlockSpec((None,) + arr.shape[1:], lambda *_: idx, pipeline_mode=pl.Buffered(1))


def _slab_bytes(arrs):
    return sum(_nbytes(a.shape[1:], a.dtype) for a in arrs)


def _dot(a, b):
    return jnp.dot(a, b, preferred_element_type=_F32)


def _rms_scale(x, n):
    ss = jnp.sum(x * x, axis=-1, keepdims=True)
    return lax.rsqrt(ss / n + NORM_EPS)


def _mean_sq(x, m):
    return _dot((x * x).astype(_BF16), m)


def _row_mean_sq(x, ones, n):
    sq = x * x
    acc = sq[:, :V7X_LANES]
    for c in range(V7X_LANES, x.shape[1], V7X_LANES):
        acc = acc + sq[:, c:c + V7X_LANES]
    return _dot(acc.astype(_BF16), ones) * (1.0 / n)


def _lane_tile(r, width):
    return jnp.concatenate([r] * (width // r.shape[1]), axis=1)


def _rope_table_kernel(pos_ref, invf_ref, cos_ref, sin_ref):
    ang = pos_ref[...].astype(_F32) * invf_ref[...]
    cos_ref[...] = jnp.cos(ang)
    sin_ref[...] = jnp.sin(ang)


def _rope_tables(positions):
    B, S = positions.shape
    T = B * S
    inv_freq = ROPE_BASE ** (-(jnp.arange(0, QK_ROPE_DIM, 2, dtype=_F32) / QK_ROPE_DIM))
    invf = jnp.tile(inv_freq, ROPE_PACK).reshape(1, V7X_LANES)
    pos = jnp.repeat(positions.reshape(T // ROPE_PACK, ROPE_PACK), ROPE_HALF, axis=1)
    rows = min(ROPE_TABLE_ROWS, T // ROPE_PACK)
    out = jax.ShapeDtypeStruct((T // ROPE_PACK, V7X_LANES), _F32)
    spec = pl.BlockSpec((rows, V7X_LANES), lambda i: (i, 0))
    cos, sin = pl.pallas_call(
        _rope_table_kernel,
        grid=(T // ROPE_PACK // rows,),
        in_specs=[spec, pl.BlockSpec((1, V7X_LANES), lambda i: (0, 0))],
        out_specs=[spec, spec],
        out_shape=[out, out],
        compiler_params=pltpu.CompilerParams(dimension_semantics=("parallel",)),
        name="rope_tables",
    )(pos, invf)
    cos, sin = cos.reshape(B, S, ROPE_HALF), sin.reshape(B, S, ROPE_HALF)
    zeros = jnp.zeros_like(cos)
    return (jnp.concatenate([cos, zeros, cos, zeros], axis=-1),
            jnp.concatenate([-sin, zeros, sin, zeros], axis=-1))


def _mla_proj_rows(r0, h_ref, cos_ref, sin_ref, gmix_ref, wdown_ref, gql_ref, gkvl_ref, wuq_ref,
                   wukv_ref, gq_ref, gkn_ref, gkr_ref, ones_ref, mq_ref, q_ref, k_ref, v_ref):
    rs = slice(r0, r0 + PROJ_SUB_ROWS)
    ones = ones_ref[...]
    h = h_ref[0, rs, :]
    r_h = lax.rsqrt(_row_mean_sq(h, ones, D_MODEL) + NORM_EPS)
    lat = _dot((h * gmix_ref[...]).astype(_BF16), wdown_ref[...])
    yield

    def latent_scale(x, n):
        return r_h * lax.rsqrt(r_h * r_h * _row_mean_sq(x, ones, n) + NORM_EPS)

    c_q = lat[:, :Q_LORA_RANK]
    c_kv = lat[:, Q_LORA_RANK:Q_LORA_RANK + KV_LORA_RANK]
    k_rope = lat[:, Q_LORA_RANK + KV_LORA_RANK:]
    r_cq = latent_scale(c_q, Q_LORA_RANK)
    c_q = (c_q * _lane_tile(r_cq, Q_LORA_RANK) * gql_ref[...]).astype(_BF16)
    r_ckv = latent_scale(c_kv, KV_LORA_RANK)
    c_kv = (c_kv * _lane_tile(r_ckv, KV_LORA_RANK) * gkvl_ref[...]).astype(_BF16)

    cos = cos_ref[0, rs, :]
    sin_signed = sin_ref[0, rs, :]

    def rope(x):
        return x * cos + pltpu.roll(x, ROPE_PAD // 2, 1) * sin_signed

    r_kr = latent_scale(k_rope, QK_ROPE_DIM)
    k_rope = rope(k_rope * r_kr * gkr_ref[...]).astype(_BF16)

    q = _dot(c_q, wuq_ref[...])
    kv = _dot(c_kv, wukv_ref[...])
    yield
    gq = gq_ref[...]
    gkn = gkn_ref[...]
    for hd in range(MLA_HEADS):
        xq = q[:, hd * QK_PAD_DIM:(hd + 1) * QK_PAD_DIM]
        yq = xq * lax.rsqrt(_mean_sq(xq, mq_ref[...]) + NORM_EPS) * gq
        q_ref[0, hd, rs, :QK_NOPE_DIM] = yq[:, :QK_NOPE_DIM].astype(_BF16)
        q_ref[0, hd, rs, QK_NOPE_DIM:] = rope(yq[:, QK_NOPE_DIM:]).astype(_BF16)
        k_ref[0, hd, rs, QK_NOPE_DIM:] = k_rope
        v0 = MLA_HEADS * QK_NOPE_DIM + hd * V_HEAD_DIM
        v_ref[0, hd, rs, :] = kv[:, v0:v0 + V_HEAD_DIM].astype(_BF16)
    for hd in range(MLA_HEADS):
        xk = kv[:, hd * QK_NOPE_DIM:(hd + 1) * QK_NOPE_DIM]
        yk = (xk * _rms_scale(xk, QK_NOPE_DIM) * gkn).astype(_BF16)
        k_ref[0, hd, rs, :QK_NOPE_DIM] = yk
    yield


def _mla_proj_kernel(*refs):
    tiles = [_mla_proj_rows(r0, *refs) for r0 in range(0, PROJ_ROWS, PROJ_SUB_ROWS)]
    n_stages = 3
    for step in range(n_stages + len(tiles) - 1):
        for t, tile in enumerate(tiles):
            if 0 <= step - t < n_stages:
                next(tile)


def _mla_proj(h, cos, sin, layer_ops, shared_ops):
    B, S, D = h.shape
    rows = PROJ_ROWS
    H = MLA_HEADS
    ops = layer_ops + shared_ops
    resident = _slab_bytes([a for a, _ in ops])
    streamed = (_nbytes((rows, D), _F32) + 2 * _nbytes((rows, ROPE_PAD), _F32)
                + 2 * _nbytes((H, rows, QK_PAD_DIM), _BF16) + _nbytes((H, rows, V_HEAD_DIM), _BF16))
    scratch = 4 * _nbytes((rows, H * QK_PAD_DIM), _F32)
    return pl.pallas_call(
        _mla_proj_kernel,
        grid=(B, S // rows),
        in_specs=[pl.BlockSpec((1, rows, D), lambda b, i: (b, i, 0)),
                  pl.BlockSpec((1, rows, ROPE_PAD), lambda b, i: (b, i, 0)),
                  pl.BlockSpec((1, rows, ROPE_PAD), lambda b, i: (b, i, 0))]
                 + [_layer_spec(a, l) for a, l in ops],
        out_specs=[pl.BlockSpec((1, H, rows, QK_PAD_DIM), lambda b, i: (b, 0, i, 0)),
                   pl.BlockSpec((1, H, rows, QK_PAD_DIM), lambda b, i: (b, 0, i, 0)),
                   pl.BlockSpec((1, H, rows, V_HEAD_DIM), lambda b, i: (b, 0, i, 0))],
        out_shape=[jax.ShapeDtypeStruct((B, H, S, QK_PAD_DIM), _BF16),
                   jax.ShapeDtypeStruct((B, H, S, QK_PAD_DIM), _BF16),
                   jax.ShapeDtypeStruct((B, H, S, V_HEAD_DIM), _BF16)],
        compiler_params=pltpu.CompilerParams(
            dimension_semantics=("parallel", "parallel"),
            vmem_limit_bytes=_vmem_limit(resident, streamed, scratch)),
        name="mla_proj",
    )(h, cos, sin, *[a for a, _ in ops])


def _attn_kernel(q_ref, k_ref, v_ref, o_ref, vext_ref):
    S = q_ref.shape[2]
    tq = ATTN_Q_ROWS
    for hd in range(ATTN_HEADS_PER_STEP):
        vext_ref[hd, :, :V_HEAD_DIM] = v_ref[0, hd]
        vext_ref[hd, :, V_HEAD_DIM:] = jnp.ones((S, V_EXT_DIM - V_HEAD_DIM), _BF16)
    row = lax.broadcasted_iota(jnp.int32, (tq, tq), 0)
    col = lax.broadcasted_iota(jnp.int32, (tq, tq), 1)
    diag_mask = col <= row
    nt_dims = (((1,), (1,)), ((), ()))

    def scores(j, hd):
        q0, kend = j * tq, (j + 1) * tq
        q = q_ref[0, hd, q0:kend, :]
        s_diag = lax.dot_general(q, k_ref[0, hd, q0:kend, :], nt_dims, preferred_element_type=_F32)
        s_diag = jnp.where(diag_mask, s_diag, -jnp.inf)
        s_past = None
        if j:
            s_past = lax.dot_general(q, k_ref[0, hd, :q0, :], nt_dims, preferred_element_type=_F32)
        return s_past, s_diag

    def finish(j, hd, s_past, s_diag):
        q0, kend = j * tq, (j + 1) * tq
        m = jnp.max(s_diag, axis=-1, keepdims=True)
        o = None
        if j:
            m = jnp.maximum(m, jnp.max(s_past, axis=-1, keepdims=True))
            o = _dot(jnp.exp2(s_past - m).astype(_BF16), vext_ref[hd, :q0, :])
        o_diag = _dot(jnp.exp2(s_diag - m).astype(_BF16), vext_ref[hd, q0:kend, :])
        o = o_diag if o is None else o + o_diag
        o_ref[0, q0:kend, hd * V_HEAD_DIM:(hd + 1) * V_HEAD_DIM] = (
            o[:, :V_HEAD_DIM] / o[:, V_HEAD_DIM:]).astype(_BF16)

    units = [(j, hd) for j in range(S // tq) for hd in range(ATTN_HEADS_PER_STEP)]
    ahead = ATTN_SCORE_LOOKAHEAD
    pending = [scores(*u) for u in units[:ahead]]
    for i, unit in enumerate(units):
        if i + ahead < len(units):
            pending.append(scores(*units[i + ahead]))
        finish(*unit, *pending.pop(0))


def _attention(q, k, v):
    B, H, S, _ = q.shape
    hps = ATTN_HEADS_PER_STEP
    streamed = hps * (2 * _nbytes((S, QK_PAD_DIM), _BF16) + 2 * _nbytes((S, V_HEAD_DIM), _BF16))
    scratch = 8 * _nbytes((ATTN_Q_ROWS, S), _F32) + hps * _nbytes((S, V_EXT_DIM), _BF16)
    return pl.pallas_call(
        _attn_kernel,
        grid=(B, H // hps),
        in_specs=[pl.BlockSpec((1, hps, S, QK_PAD_DIM), lambda b, h: (b, h, 0, 0)),
                  pl.BlockSpec((1, hps, S, QK_PAD_DIM), lambda b, h: (b, h, 0, 0)),
                  pl.BlockSpec((1, hps, S, V_HEAD_DIM), lambda b, h: (b, h, 0, 0))],
        out_specs=pl.BlockSpec((1, S, hps * V_HEAD_DIM), lambda b, h: (b, 0, h)),
        out_shape=jax.ShapeDtypeStruct((B, S, H * V_HEAD_DIM), _BF16),
        scratch_shapes=[pltpu.VMEM((hps, S, V_EXT_DIM), _BF16)],
        compiler_params=pltpu.CompilerParams(
            dimension_semantics=("parallel", "parallel"),
            vmem_limit_bytes=_vmem_limit(0, streamed, scratch)),
        name="mla_attention",
    )(q, k, v)


def _post_kernel(mix_ref, h_ref, p_ref, wmix_ref, gffn_ref, wup_ref, wdn_ref, gple_ref,
                 wgate_ref, wproj_ref, out_ref):
    h = h_ref[...] + _dot(mix_ref[...], wmix_ref[...])
    hn = (h * _rms_scale(h, D_MODEL) * gffn_ref[...]).astype(_BF16)
    for c in range(D_FF // FF_CHUNK):
        up = _dot(hn, wup_ref[:, c * FF_CHUNK:(c + 1) * FF_CHUNK])
        act = jnp.square(jnp.maximum(up, 0.0)).astype(_BF16)
        h = h + _dot(act, wdn_ref[c * FF_CHUNK:(c + 1) * FF_CHUNK, :])
    hn = (h * _rms_scale(h, D_MODEL) * gple_ref[...]).astype(_BF16)
    gate = jax.nn.sigmoid(_dot(hn, wgate_ref[...]))
    proj = _dot(p_ref[...].astype(_BF16), wproj_ref[...])
    out_ref[...] = h + gate * proj


def _post(mix, h, p, layer, ops):
    T, D = h.shape
    rows = POST_ROWS
    kmix = mix.shape[1]
    resident = _slab_bytes([a for a, _ in ops])
    streamed = (_nbytes((rows, kmix), _BF16) + 2 * _nbytes((rows, D), _F32)
                + _nbytes((rows, PLE_DIM), _F32))
    scratch = 2 * _nbytes((rows, FF_CHUNK), _F32) + 4 * _nbytes((rows, D), _F32)
    return pl.pallas_call(
        _post_kernel,
        grid=(T // rows,),
        in_specs=[pl.BlockSpec((rows, kmix), lambda i: (i, 0)),
                  pl.BlockSpec((rows, D), lambda i: (i, 0)),
                  pl.BlockSpec((None, rows, PLE_DIM), lambda i: (layer, i, 0))]
                 + [_layer_spec(a, l) for a, l in ops],
        out_specs=pl.BlockSpec((rows, D), lambda i: (i, 0)),
        out_shape=jax.ShapeDtypeStruct((T, D), _F32),
        compiler_params=pltpu.CompilerParams(
            dimension_semantics=("parallel",),
            vmem_limit_bytes=_vmem_limit(resident, streamed, scratch)),
        name="post_ffn_ple",
    )(mix, h, p, *[a for a, _ in ops])


_GELU_C = 0.7978845608028654
_GELU_A = 0.044715


def _gelu_tanh(x):
    half_x = 0.5 * x
    inner = x * (_GELU_C + (_GELU_C * _GELU_A) * (x * x))
    return half_x + half_x * jnp.tanh(inner)


def _gmlp_rows(row0, ws_masked, h_ref, gmix_ref, win_ref, lng_ref, lnb_ref, bst_ref, y_ref):
    h = h_ref[row0:row0 + GMLP_SUB_ROWS, :]
    hn = (h * _rms_scale(h, D_MODEL) * gmix_ref[...]).astype(_BF16)
    v = _gelu_tanh(_dot(hn, win_ref[:, GMLP_HALF:2 * GMLP_HALF]))
    mu = jnp.mean(v, axis=-1, keepdims=True)
    vc = v - mu
    var = jnp.mean(vc * vc, axis=-1, keepdims=True)
    vn = (vc * lax.rsqrt(var + NORM_EPS) * lng_ref[...] + lnb_ref[...]).astype(_BF16)
    u = _gelu_tanh(_dot(hn, win_ref[:, :GMLP_HALF]))
    for g in range(GMLP_GROUPS):
        bias = bst_ref[:, g:g + 1]
        c0 = g * GMLP_GROUP_DIM
        for n in range(GMLP_SUB_ROWS // GMLP_CHUNK):
            r0 = n * GMLP_CHUNK
            sv = _dot(ws_masked[g], vn[r0:r0 + GMLP_CHUNK, c0:c0 + GMLP_GROUP_DIM]) + bias
            y_ref[row0 + r0:row0 + r0 + GMLP_CHUNK, c0:c0 + GMLP_GROUP_DIM] = (
                u[r0:r0 + GMLP_CHUNK, c0:c0 + GMLP_GROUP_DIM] * sv).astype(_BF16)


def _gmlp_kernel(h_ref, gmix_ref, win_ref, lng_ref, lnb_ref, ws_ref, bst_ref, y_ref):
    t_idx = lax.broadcasted_iota(jnp.int32, (GMLP_CHUNK, GMLP_CHUNK), 0)
    s_idx = lax.broadcasted_iota(jnp.int32, (GMLP_CHUNK, GMLP_CHUNK), 1)
    causal = s_idx <= t_idx
    ws_masked = [jnp.where(causal, ws_ref[g], 0.0).astype(_BF16) for g in range(GMLP_GROUPS)]
    for row0 in range(0, GMLP_ROWS, GMLP_SUB_ROWS):
        _gmlp_rows(row0, ws_masked, h_ref, gmix_ref, win_ref, lng_ref, lnb_ref, bst_ref, y_ref)


def _gmlp_front(h, ops):
    T, D = h.shape
    rows = GMLP_ROWS
    resident = _slab_bytes([a for a, _ in ops])
    streamed = _nbytes((rows, D), _F32) + _nbytes((rows, GMLP_HALF), _BF16)
    scratch = 6 * _nbytes((rows, GMLP_HALF), _F32)
    return pl.pallas_call(
        _gmlp_kernel,
        grid=(T // rows,),
        in_specs=[pl.BlockSpec((rows, D), lambda i: (i, 0))] + [_layer_spec(a, l) for a, l in ops],
        out_specs=pl.BlockSpec((rows, GMLP_HALF), lambda i: (i, 0)),
        out_shape=jax.ShapeDtypeStruct((T, GMLP_HALF), _BF16),
        compiler_params=pltpu.CompilerParams(
            dimension_semantics=("parallel",),
            vmem_limit_bytes=_vmem_limit(resident, streamed, scratch)),
        name="gmlp_front",
    )(h, *[a for a, _ in ops])


def _rope_pad(w):
    z = jnp.zeros(w.shape[:-1] + (ROPE_HALF,), w.dtype)
    return jnp.concatenate([w[..., :ROPE_HALF], z, w[..., ROPE_HALF:], z], axis=-1)


def _skew_pitch(w):
    zeros = jnp.zeros(w.shape[:-1] + (V7X_LANES,), _BF16)
    return jnp.concatenate([w.astype(_BF16), zeros], axis=-1)


def _rows(g):
    return g.reshape(g.shape[0], 1, g.shape[-1]).astype(_F32)


def _mla_params(w_down, w_uq, w_ukv, q_nope_g, q_rope_g, k_nope_g, k_rope_g):
    H = MLA_HEADS
    L = w_down.shape[0]
    n_lat = Q_LORA_RANK + KV_LORA_RANK
    wdown = jnp.concatenate([w_down[..., :n_lat], _rope_pad(w_down[..., n_lat:])], axis=-1)
    uq = w_uq.reshape(L, Q_LORA_RANK, H, QK_NOPE_DIM + QK_ROPE_DIM)
    wuq = jnp.concatenate([uq[..., :QK_NOPE_DIM], _rope_pad(uq[..., QK_NOPE_DIM:])], axis=-1)
    wuq = wuq.reshape(L, Q_LORA_RANK, H * QK_PAD_DIM)
    ukv = w_ukv.reshape(L, KV_LORA_RANK, H, QK_NOPE_DIM + V_HEAD_DIM)
    wukv = jnp.concatenate([ukv[..., :QK_NOPE_DIM].reshape(L, KV_LORA_RANK, H * QK_NOPE_DIM),
                            ukv[..., QK_NOPE_DIM:].reshape(L, KV_LORA_RANK, H * V_HEAD_DIM)], axis=-1)
    q_scale = (QK_NOPE_DIM + QK_ROPE_DIM) ** -0.5 * math.log2(math.e)
    gq = _rows(jnp.concatenate([q_nope_g, _rope_pad(q_rope_g)], axis=-1)) * q_scale
    gkn = _rows(k_nope_g)
    return (wdown.astype(_BF16), wuq.astype(_BF16), wukv.astype(_BF16), gq, gkn,
            _rows(_rope_pad(k_rope_g)))


def _group_mean_matrix(group_sizes, counts):
    blocks = []
    width = sum(group_sizes)
    off = 0
    for size, count in zip(group_sizes, counts):
        col = jnp.zeros((size, width), _F32).at[:, off:off + size].set(1.0 / count)
        blocks.append(col)
        off += size
    return jnp.concatenate(blocks, axis=0).astype(_BF16)[None]


def _stat_consts():
    ones = jnp.ones((1, V7X_LANES, V7X_LANES), _BF16)
    mq = _group_mean_matrix((QK_NOPE_DIM, ROPE_PAD), (QK_NOPE_DIM, QK_ROPE_DIM))
    return [(c, 0) for c in (ones, mq)]


def kernel(x, p, positions, norm_mix, norm_ffn, norm_ple, mla_w_down, mla_q_lora_g, mla_kv_lora_g,
           mla_w_uq, mla_w_ukv, mla_q_nope_g, mla_q_rope_g, mla_k_nope_g, mla_k_rope_g, mla_w_out,
           gmlp_w_in, gmlp_ln_g, gmlp_ln_b, gmlp_w_s, gmlp_b_s, gmlp_w_out, ffn_w_up, ffn_w_down,
           ple_w_gate, ple_w_proj):
    B, S, D = x.shape
    T = B * S
    p_flat = p.reshape(DEPTH, T, PLE_DIM)
    g_mix, g_ffn, g_ple = _rows(norm_mix), _rows(norm_ffn), _rows(norm_ple)
    wdown, wuq, wukv, gq, gkn, gkr = _mla_params(mla_w_down, mla_w_uq, mla_w_ukv, mla_q_nope_g,
                                                 mla_q_rope_g, mla_k_nope_g, mla_k_rope_g)
    gql, gkvl = _rows(mla_q_lora_g), _rows(mla_kv_lora_g)
    mla_wout = mla_w_out.astype(_BF16)
    gmlp_win, gmlp_wout = _skew_pitch(gmlp_w_in), gmlp_w_out.astype(_BF16)
    ln_g, ln_b = _rows(gmlp_ln_g), _rows(gmlp_ln_b)
    bst = jnp.swapaxes(gmlp_b_s, 1, 2)
    w_up, w_dn = ffn_w_up.astype(_BF16), ffn_w_down.astype(_BF16)
    w_gate, w_proj = ple_w_gate.astype(_BF16), ple_w_proj.astype(_BF16)
    shared = _stat_consts()
    cos, sin = _rope_tables(positions)

    h = x.reshape(T, D)
    for i in range(DEPTH):
        j = i // 2
        if i % 2 == 0:
            layer_ops = [(g_mix, i), (wdown, j), (gql, j), (gkvl, j), (wuq, j), (wukv, j), (gq, j),
                         (gkn, j), (gkr, j)]
            q, k, v = _mla_proj(h.reshape(B, S, D), cos, sin, layer_ops, shared)
            mix = _attention(q, k, v).reshape(T, MLA_HEADS * V_HEAD_DIM)
            wmix = (mla_wout, j)
        else:
            mix = _gmlp_front(h, [(g_mix, i), (gmlp_win, j), (ln_g, j), (ln_b, j), (gmlp_w_s, j),
                                  (bst, j)])
            wmix = (gmlp_wout, j)
        h = _post(mix, h, p_flat, i, [wmix, (g_ffn, i), (w_up, i), (w_dn, i), (g_ple, i),
                                      (w_gate, i), (w_proj, i)])
    return h.reshape(B, S, D)
```

```python
import math

import jax
import jax.numpy as jnp
from jax import lax
from jax.experimental import pallas as pl
from jax.experimental.pallas import tpu as pltpu

D_MODEL = 1024
DEPTH = 4
MLA_HEADS = 8
QK_NOPE_DIM = 128
QK_ROPE_DIM = 64
V_HEAD_DIM = 128
Q_LORA_RANK = 384
KV_LORA_RANK = 256
ROPE_BASE = 10000.0
GMLP_CHUNK = 128
GMLP_HALF = 2 * D_MODEL
GMLP_GROUPS = 8
GMLP_GROUP_DIM = GMLP_HALF // GMLP_GROUPS
D_FF = 4 * D_MODEL
PLE_DIM = 256
NORM_EPS = 1e-6

V7X_LANES = 128
V7X_VMEM_REQUEST_CAP = 56 << 20

ROPE_HALF = QK_ROPE_DIM // 2
ROPE_PAD = V7X_LANES
ROPE_PACK = V7X_LANES // ROPE_HALF
QK_PAD_DIM = QK_NOPE_DIM + ROPE_PAD
LAT_PAD_DIM = Q_LORA_RANK + KV_LORA_RANK + ROPE_PAD
V_EXT_DIM = 2 * V_HEAD_DIM

PROJ_ROWS = 1024
PROJ_SUB_ROWS = 256
ROPE_TABLE_ROWS = 1024
ATTN_Q_ROWS = 256
ATTN_SCORE_LOOKAHEAD = 1
ATTN_HEADS_PER_STEP = 4
POST_ROWS = 512
FF_CHUNK = 1024
SKEW_COPY_ROWS = 256
GMLP_ROWS = 512
GMLP_SUB_ROWS = 256

_BF16 = jnp.bfloat16
_F32 = jnp.float32


def _vmem_limit(resident_bytes, streamed_bytes, scratch_bytes):
    need = resident_bytes + 2 * streamed_bytes + scratch_bytes
    return min(V7X_VMEM_REQUEST_CAP, need)


def _nbytes(shape, dtype):
    return math.prod(shape) * jnp.dtype(dtype).itemsize


def _layer_spec(arr, layer):
    idx = (layer,) + (0,) * (arr.ndim - 1)
    return pl.BlockSpec((None,) + arr.shape[1:], lambda *_: idx, pipeline_mode=pl.Buffered(1))


def _slab_bytes(arrs):
    return sum(_nbytes(a.shape[1:], a.dtype) for a in arrs)


def _dot(a, b):
    return jnp.dot(a, b, preferred_element_type=_F32)


def _rms_scale(x, n):
    ss = jnp.sum(x * x, axis=-1, keepdims=True)
    return lax.rsqrt(ss / n + NORM_EPS)


def _mean_sq(x, m):
    return _dot((x * x).astype(_BF16), m)


def _row_mean_sq(x, ones, n):
    sq = x * x
    acc = sq[:, :V7X_LANES]
    for c in range(V7X_LANES, x.shape[1], V7X_LANES):
        acc = acc + sq[:, c:c + V7X_LANES]
    return _dot(acc.astype(_BF16), ones) * (1.0 / n)


def _lane_tile(r, width):
    return jnp.concatenate([r] * (width // r.shape[1]), axis=1)


def _rope_table_kernel(pos_ref, invf_ref, cos_ref, sin_ref):
    ang = pos_ref[...].astype(_F32) * invf_ref[...]
    cos_packed, sin_packed = jnp.cos(ang), jnp.sin(ang)
    rows = ang.shape[0]
    low = lax.broadcasted_iota(jnp.int32, ang.shape, 1) < ROPE_HALF
    for g in range(ROPE_PACK):
        shift = (V7X_LANES - g * ROPE_HALF) % V7X_LANES
        c = jnp.where(low, pltpu.roll(cos_packed, shift, 1) if shift else cos_packed, 0.0)
        s = jnp.where(low, pltpu.roll(sin_packed, shift, 1) if shift else sin_packed, 0.0)
        out_rows = pl.ds(g, rows, stride=ROPE_PACK)
        cos_ref[out_rows, :] = c + pltpu.roll(c, ROPE_PAD // 2, 1)
        sin_ref[out_rows, :] = pltpu.roll(s, ROPE_PAD // 2, 1) - s


def _rope_tables(positions):
    B, S = positions.shape
    T = B * S
    inv_freq = ROPE_BASE ** (-(jnp.arange(0, QK_ROPE_DIM, 2, dtype=_F32) / QK_ROPE_DIM))
    invf = jnp.tile(inv_freq, ROPE_PACK).reshape(1, V7X_LANES)
    pos = jnp.repeat(positions.reshape(T // ROPE_PACK, ROPE_PACK), ROPE_HALF, axis=1)
    rows = min(ROPE_TABLE_ROWS, T // ROPE_PACK)
    out = jax.ShapeDtypeStruct((T, ROPE_PAD), _F32)
    out_spec = pl.BlockSpec((rows * ROPE_PACK, ROPE_PAD), lambda i: (i, 0))
    cos, sin = pl.pallas_call(
        _rope_table_kernel,
        grid=(T // ROPE_PACK // rows,),
        in_specs=[pl.BlockSpec((rows, V7X_LANES), lambda i: (i, 0)),
                  pl.BlockSpec((1, V7X_LANES), lambda i: (0, 0))],
        out_specs=[out_spec, out_spec],
        out_shape=[out, out],
        compiler_params=pltpu.CompilerParams(dimension_semantics=("parallel",)),
        name="rope_tables",
    )(pos, invf)
    return cos.reshape(B, S, ROPE_PAD), sin.reshape(B, S, ROPE_PAD)


def _mla_proj_rows(r0, h_ref, cos_ref, sin_ref, gmix_ref, wdown_ref, gql_ref, gkvl_ref, wuq_ref,
                   wukv_ref, gq_ref, gkn_ref, gkr_ref, ones_ref, mq_ref, q_ref, k_ref, v_ref):
    rs = slice(r0, r0 + PROJ_SUB_ROWS)
    ones = ones_ref[...]
    h = h_ref[0, rs, :]
    r_h = lax.rsqrt(_row_mean_sq(h, ones, D_MODEL) + NORM_EPS)
    lat = _dot((h * gmix_ref[...]).astype(_BF16), wdown_ref[...])
    yield

    def latent_scale(x, n):
        return r_h * lax.rsqrt(r_h * r_h * _row_mean_sq(x, ones, n) + NORM_EPS)

    c_q = lat[:, :Q_LORA_RANK]
    c_kv = lat[:, Q_LORA_RANK:Q_LORA_RANK + KV_LORA_RANK]
    k_rope = lat[:, Q_LORA_RANK + KV_LORA_RANK:]
    r_cq = latent_scale(c_q, Q_LORA_RANK)
    c_q = (c_q * _lane_tile(r_cq, Q_LORA_RANK) * gql_ref[...]).astype(_BF16)
    r_ckv = latent_scale(c_kv, KV_LORA_RANK)
    c_kv = (c_kv * _lane_tile(r_ckv, KV_LORA_RANK) * gkvl_ref[...]).astype(_BF16)

    cos = cos_ref[0, rs, :]
    sin_signed = sin_ref[0, rs, :]

    def rope(x):
        return x * cos + pltpu.roll(x, ROPE_PAD // 2, 1) * sin_signed

    r_kr = latent_scale(k_rope, QK_ROPE_DIM)
    k_rope = rope(k_rope * r_kr * gkr_ref[...]).astype(_BF16)

    q = _dot(c_q, wuq_ref[...])
    kv = _dot(c_kv, wukv_ref[...])
    yield
    gq = gq_ref[...]
    gkn = gkn_ref[...]
    for hd in range(MLA_HEADS):
        xq = q[:, hd * QK_PAD_DIM:(hd + 1) * QK_PAD_DIM]
        yq = xq * lax.rsqrt(_mean_sq(xq, mq_ref[...]) + NORM_EPS) * gq
        q_ref[0, hd, rs, :QK_NOPE_DIM] = yq[:, :QK_NOPE_DIM].astype(_BF16)
        q_ref[0, hd, rs, QK_NOPE_DIM:] = rope(yq[:, QK_NOPE_DIM:]).astype(_BF16)
        k_ref[0, hd, rs, QK_NOPE_DIM:] = k_rope
        v0 = MLA_HEADS * QK_NOPE_DIM + hd * V_HEAD_DIM
        v_ref[0, hd, rs, :] = kv[:, v0:v0 + V_HEAD_DIM].astype(_BF16)
        xk = kv[:, hd * QK_NOPE_DIM:(hd + 1) * QK_NOPE_DIM]
        yk = (xk * _rms_scale(xk, QK_NOPE_DIM) * gkn).astype(_BF16)
        k_ref[0, hd, rs, :QK_NOPE_DIM] = yk
    yield


def _mla_proj_kernel(*refs):
    tiles = [_mla_proj_rows(r0, *refs) for r0 in range(0, PROJ_ROWS, PROJ_SUB_ROWS)]
    n_stages = 3
    for step in range(n_stages + len(tiles) - 1):
        for t, tile in enumerate(tiles):
            if 0 <= step - t < n_stages:
                next(tile)


def _mla_proj(h, cos, sin, layer_ops, shared_ops):
    B, S, D = h.shape
    rows = PROJ_ROWS
    H = MLA_HEADS
    ops = layer_ops + shared_ops
    resident = _slab_bytes([a for a, _ in ops])
    streamed = (_nbytes((rows, D), _F32) + 2 * _nbytes((rows, ROPE_PAD), _F32)
                + 2 * _nbytes((H, rows, QK_PAD_DIM), _BF16) + _nbytes((H, rows, V_HEAD_DIM), _BF16))
    scratch = 4 * _nbytes((rows, H * QK_PAD_DIM), _F32)
    return pl.pallas_call(
        _mla_proj_kernel,
        grid=(B, S // rows),
        in_specs=[pl.BlockSpec((1, rows, D), lambda b, i: (b, i, 0)),
                  pl.BlockSpec((1, rows, ROPE_PAD), lambda b, i: (b, i, 0)),
                  pl.BlockSpec((1, rows, ROPE_PAD), lambda b, i: (b, i, 0))]
                 + [_layer_spec(a, l) for a, l in ops],
        out_specs=[pl.BlockSpec((1, H, rows, QK_PAD_DIM), lambda b, i: (b, 0, i, 0)),
                   pl.BlockSpec((1, H, rows, QK_PAD_DIM), lambda b, i: (b, 0, i, 0)),
                   pl.BlockSpec((1, H, rows, V_HEAD_DIM), lambda b, i: (b, 0, i, 0))],
        out_shape=[jax.ShapeDtypeStruct((B, H, S, QK_PAD_DIM), _BF16),
                   jax.ShapeDtypeStruct((B, H, S, QK_PAD_DIM), _BF16),
                   jax.ShapeDtypeStruct((B, H, S, V_HEAD_DIM), _BF16)],
        compiler_params=pltpu.CompilerParams(
            dimension_semantics=("parallel", "parallel"),
            vmem_limit_bytes=_vmem_limit(resident, streamed, scratch)),
        name="mla_proj",
    )(h, cos, sin, *[a for a, _ in ops])


def _attn_kernel(q_ref, k_ref, v_ref, o_ref, vext_ref):
    S = q_ref.shape[2]
    tq = ATTN_Q_ROWS
    for hd in range(ATTN_HEADS_PER_STEP):
        vext_ref[hd, :, :V_HEAD_DIM] = v_ref[0, hd]
        vext_ref[hd, :, V_HEAD_DIM:] = jnp.ones((S, V_EXT_DIM - V_HEAD_DIM), _BF16)
    row = lax.broadcasted_iota(jnp.int32, (tq, tq), 0)
    col = lax.broadcasted_iota(jnp.int32, (tq, tq), 1)
    diag_mask = col <= row
    nt_dims = (((1,), (1,)), ((), ()))

    def scores(j, hd):
        q0, kend = j * tq, (j + 1) * tq
        q = q_ref[0, hd, q0:kend, :]
        s_diag = lax.dot_general(q, k_ref[0, hd, q0:kend, :], nt_dims, preferred_element_type=_F32)
        s_diag = jnp.where(diag_mask, s_diag, -jnp.inf)
        s_past = None
        if j:
            s_past = lax.dot_general(q, k_ref[0, hd, :q0, :], nt_dims, preferred_element_type=_F32)
        return s_past, s_diag

    def finish(j, hd, s_past, s_diag):
        q0, kend = j * tq, (j + 1) * tq
        m = jnp.max(s_diag, axis=-1, keepdims=True)
        o = None
        if j:
            m = jnp.maximum(m, jnp.max(s_past, axis=-1, keepdims=True))
            o = _dot(jnp.exp2(s_past - m).astype(_BF16), vext_ref[hd, :q0, :])
        o_diag = _dot(jnp.exp2(s_diag - m).astype(_BF16), vext_ref[hd, q0:kend, :])
        o = o_diag if o is None else o + o_diag
        o_ref[0, q0:kend, hd * V_HEAD_DIM:(hd + 1) * V_HEAD_DIM] = (
            o[:, :V_HEAD_DIM] / o[:, V_HEAD_DIM:]).astype(_BF16)

    units = [(j, hd) for j in range(S // tq) for hd in range(ATTN_HEADS_PER_STEP)]
    ahead = ATTN_SCORE_LOOKAHEAD
    pending = [scores(*u) for u in units[:ahead]]
    for i, unit in enumerate(units):
        if i + ahead < len(units):
            pending.append(scores(*units[i + ahead]))
        finish(*unit, *pending.pop(0))


def _attention(q, k, v):
    B, H, S, _ = q.shape
    hps = ATTN_HEADS_PER_STEP
    streamed = hps * (2 * _nbytes((S, QK_PAD_DIM), _BF16) + 2 * _nbytes((S, V_HEAD_DIM), _BF16))
    scratch = 8 * _nbytes((ATTN_Q_ROWS, S), _F32) + hps * _nbytes((S, V_EXT_DIM), _BF16)
    return pl.pallas_call(
        _attn_kernel,
        grid=(B, H // hps),
        in_specs=[pl.BlockSpec((1, hps, S, QK_PAD_DIM), lambda b, h: (b, h, 0, 0)),
                  pl.BlockSpec((1, hps, S, QK_PAD_DIM), lambda b, h: (b, h, 0, 0)),
                  pl.BlockSpec((1, hps, S, V_HEAD_DIM), lambda b, h: (b, h, 0, 0))],
        out_specs=pl.BlockSpec((1, S, hps * V_HEAD_DIM), lambda b, h: (b, 0, h)),
        out_shape=jax.ShapeDtypeStruct((B, S, H * V_HEAD_DIM), _BF16),
        scratch_shapes=[pltpu.VMEM((hps, S, V_EXT_DIM), _BF16)],
        compiler_params=pltpu.CompilerParams(
            dimension_semantics=("parallel", "parallel"),
            vmem_limit_bytes=_vmem_limit(0, streamed, scratch)),
        name="mla_attention",
    )(q, k, v)


def _post_kernel(mix_ref, h_ref, p_ref, wmix_ref, gffn_ref, wup_ref, wdn_ref, gple_ref,
                 wgate_ref, wproj_ref, out_ref):
    h = h_ref[...] + _dot(mix_ref[...], wmix_ref[...])
    hn = (h * _rms_scale(h, D_MODEL) * gffn_ref[...]).astype(_BF16)
    for c in range(D_FF // FF_CHUNK):
        up = _dot(hn, wup_ref[:, c * FF_CHUNK:(c + 1) * FF_CHUNK])
        act = jnp.square(jnp.maximum(up, 0.0)).astype(_BF16)
        h = h + _dot(act, wdn_ref[c * FF_CHUNK:(c + 1) * FF_CHUNK, :])
    hn = (h * _rms_scale(h, D_MODEL) * gple_ref[...]).astype(_BF16)
    gate = jax.nn.sigmoid(_dot(hn, wgate_ref[...]))
    proj = _dot(p_ref[...].astype(_BF16), wproj_ref[...])
    out_ref[...] = h + gate * proj


def _post(mix, h, p, layer, ops):
    T, D = h.shape
    rows = POST_ROWS
    kmix = mix.shape[1]
    resident = _slab_bytes([a for a, _ in ops])
    streamed = (_nbytes((rows, kmix), _BF16) + 2 * _nbytes((rows, D), _F32)
                + _nbytes((rows, PLE_DIM), _F32))
    scratch = 2 * _nbytes((rows, FF_CHUNK), _F32) + 4 * _nbytes((rows, D), _F32)
    return pl.pallas_call(
        _post_kernel,
        grid=(T // rows,),
        in_specs=[pl.BlockSpec((rows, kmix), lambda i: (i, 0)),
                  pl.BlockSpec((rows, D), lambda i: (i, 0)),
                  pl.BlockSpec((None, rows, PLE_DIM), lambda i: (layer, i, 0))]
                 + [_layer_spec(a, l) for a, l in ops],
        out_specs=pl.BlockSpec((rows, D), lambda i: (i, 0)),
        out_shape=jax.ShapeDtypeStruct((T, D), _F32),
        compiler_params=pltpu.CompilerParams(
            dimension_semantics=("parallel",),
            vmem_limit_bytes=_vmem_limit(resident, streamed, scratch)),
        name="post_ffn_ple",
    )(mix, h, p, *[a for a, _ in ops])


_GELU_C = 0.7978845608028654
_GELU_A = 0.044715


def _gelu_tanh(x):
    half_x = 0.5 * x
    inner = x * (_GELU_C + (_GELU_C * _GELU_A) * (x * x))
    return half_x + half_x * jnp.tanh(inner)


def _gmlp_rows(row0, ws_masked, h_ref, gmix_ref, win_ref, lng_ref, lnb_ref, bst_ref, y_ref):
    h = h_ref[row0:row0 + GMLP_SUB_ROWS, :]
    hn = (h * _rms_scale(h, D_MODEL) * gmix_ref[...]).astype(_BF16)
    v = _gelu_tanh(_dot(hn, win_ref[:, GMLP_HALF:2 * GMLP_HALF]))
    mu = jnp.mean(v, axis=-1, keepdims=True)
    vc = v - mu
    var = jnp.mean(vc * vc, axis=-1, keepdims=True)
    vn = (vc * lax.rsqrt(var + NORM_EPS) * lng_ref[...] + lnb_ref[...]).astype(_BF16)
    u = _gelu_tanh(_dot(hn, win_ref[:, :GMLP_HALF]))
    for g in range(GMLP_GROUPS):
        bias = bst_ref[:, g:g + 1]
        c0 = g * GMLP_GROUP_DIM
        for n in range(GMLP_SUB_ROWS // GMLP_CHUNK):
            r0 = n * GMLP_CHUNK
            sv = _dot(ws_masked[g], vn[r0:r0 + GMLP_CHUNK, c0:c0 + GMLP_GROUP_DIM]) + bias
            y_ref[row0 + r0:row0 + r0 + GMLP_CHUNK, c0:c0 + GMLP_GROUP_DIM] = (
                u[r0:r0 + GMLP_CHUNK, c0:c0 + GMLP_GROUP_DIM] * sv).astype(_BF16)


def _gmlp_kernel(h_ref, gmix_ref, win_ref, lng_ref, lnb_ref, ws_ref, bst_ref, y_ref):
    t_idx = lax.broadcasted_iota(jnp.int32, (GMLP_CHUNK, GMLP_CHUNK), 0)
    s_idx = lax.broadcasted_iota(jnp.int32, (GMLP_CHUNK, GMLP_CHUNK), 1)
    causal = s_idx <= t_idx
    ws_masked = [jnp.where(causal, ws_ref[g], 0.0).astype(_BF16) for g in range(GMLP_GROUPS)]
    for row0 in range(0, GMLP_ROWS, GMLP_SUB_ROWS):
        _gmlp_rows(row0, ws_masked, h_ref, gmix_ref, win_ref, lng_ref, lnb_ref, bst_ref, y_ref)


def _gmlp_front(h, ops):
    T, D = h.shape
    rows = GMLP_ROWS
    resident = _slab_bytes([a for a, _ in ops])
    streamed = _nbytes((rows, D), _F32) + _nbytes((rows, GMLP_HALF), _BF16)
    scratch = 6 * _nbytes((rows, GMLP_HALF), _F32)
    return pl.pallas_call(
        _gmlp_kernel,
        grid=(T // rows,),
        in_specs=[pl.BlockSpec((rows, D), lambda i: (i, 0))] + [_layer_spec(a, l) for a, l in ops],
        out_specs=pl.BlockSpec((rows, GMLP_HALF), lambda i: (i, 0)),
        out_shape=jax.ShapeDtypeStruct((T, GMLP_HALF), _BF16),
        compiler_params=pltpu.CompilerParams(
            dimension_semantics=("parallel",),
            vmem_limit_bytes=_vmem_limit(resident, streamed, scratch)),
        name="gmlp_front",
    )(h, *[a for a, _ in ops])


def _rope_pad(w):
    z = jnp.zeros(w.shape[:-1] + (ROPE_HALF,), w.dtype)
    return jnp.concatenate([w[..., :ROPE_HALF], z, w[..., ROPE_HALF:], z], axis=-1)


def _skew_pitch_kernel(w_ref, out_ref):
    n = w_ref.shape[-1]
    out_ref[:, :n] = w_ref[...].astype(_BF16)
    out_ref[:, n:] = jnp.zeros((out_ref.shape[0], out_ref.shape[1] - n), _BF16)


def _skew_pitch(w):
    L, K, N = w.shape
    rows = SKEW_COPY_ROWS
    return pl.pallas_call(
        _skew_pitch_kernel,
        grid=(L, K // rows),
        in_specs=[pl.BlockSpec((None, rows, N), lambda l, i: (l, i, 0))],
        out_specs=pl.BlockSpec((None, rows, N + V7X_LANES), lambda l, i: (l, i, 0)),
        out_shape=jax.ShapeDtypeStruct((L, K, N + V7X_LANES), _BF16),
        compiler_params=pltpu.CompilerParams(dimension_semantics=("parallel", "parallel")),
        name="skew_pitch",
    )(w)


def _rows(g):
    return g.reshape(g.shape[0], 1, g.shape[-1]).astype(_F32)


def _mla_params(w_down, w_uq, w_ukv, q_nope_g, q_rope_g, k_nope_g, k_rope_g):
    H = MLA_HEADS
    L = w_down.shape[0]
    n_lat = Q_LORA_RANK + KV_LORA_RANK
    wdown = jnp.concatenate([w_down[..., :n_lat], _rope_pad(w_down[..., n_lat:])], axis=-1)
    uq = w_uq.reshape(L, Q_LORA_RANK, H, QK_NOPE_DIM + QK_ROPE_DIM)
    wuq = jnp.concatenate([uq[..., :QK_NOPE_DIM], _rope_pad(uq[..., QK_NOPE_DIM:])], axis=-1)
    wuq = wuq.reshape(L, Q_LORA_RANK, H * QK_PAD_DIM)
    ukv = w_ukv.reshape(L, KV_LORA_RANK, H, QK_NOPE_DIM + V_HEAD_DIM)
    wukv = jnp.concatenate([ukv[..., :QK_NOPE_DIM].reshape(L, KV_LORA_RANK, H * QK_NOPE_DIM),
                            ukv[..., QK_NOPE_DIM:].reshape(L, KV_LORA_RANK, H * V_HEAD_DIM)], axis=-1)
    q_scale = (QK_NOPE_DIM + QK_ROPE_DIM) ** -0.5 * math.log2(math.e)
    gq = _rows(jnp.concatenate([q_nope_g, _rope_pad(q_rope_g)], axis=-1)) * q_scale
    gkn = _rows(k_nope_g)
    return (wdown.astype(_BF16), wuq.astype(_BF16), wukv.astype(_BF16), gq, gkn,
            _rows(_rope_pad(k_rope_g)))


def _group_mean_matrix(group_sizes, counts):
    blocks = []
    width = sum(group_sizes)
    off = 0
    for size, count in zip(group_sizes, counts):
        col = jnp.zeros((size, width), _F32).at[:, off:off + size].set(1.0 / count)
        blocks.append(col)
        off += size
    return jnp.concatenate(blocks, axis=0).astype(_BF16)[None]


def _stat_consts():
    ones = jnp.ones((1, V7X_LANES, V7X_LANES), _BF16)
    mq = _group_mean_matrix((QK_NOPE_DIM, ROPE_PAD), (QK_NOPE_DIM, QK_ROPE_DIM))
    return [(c, 0) for c in (ones, mq)]


def kernel(x, p, positions, norm_mix, norm_ffn, norm_ple, mla_w_down, mla_q_lora_g, mla_kv_lora_g,
           mla_w_uq, mla_w_ukv, mla_q_nope_g, mla_q_rope_g, mla_k_nope_g, mla_k_rope_g, mla_w_out,
           gmlp_w_in, gmlp_ln_g, gmlp_ln_b, gmlp_w_s, gmlp_b_s, gmlp_w_out, ffn_w_up, ffn_w_down,
           ple_w_gate, ple_w_proj):
    B, S, D = x.shape
    T = B * S
    p_flat = p.reshape(DEPTH, T, PLE_DIM)
    g_mix, g_ffn, g_ple = _rows(norm_mix), _rows(norm_ffn), _rows(norm_ple)
    wdown, wuq, wukv, gq, gkn, gkr = _mla_params(mla_w_down, mla_w_uq, mla_w_ukv, mla_q_nope_g,
                                                 mla_q_rope_g, mla_k_nope_g, mla_k_rope_g)
    gql, gkvl = _rows(mla_q_lora_g), _rows(mla_kv_lora_g)
    mla_wout = mla_w_out.astype(_BF16)
    gmlp_win, gmlp_wout = _skew_pitch(gmlp_w_in), gmlp_w_out.astype(_BF16)
    ln_g, ln_b = _rows(gmlp_ln_g), _rows(gmlp_ln_b)
    bst = jnp.swapaxes(gmlp_b_s, 1, 2)
    w_up, w_dn = ffn_w_up.astype(_BF16), ffn_w_down.astype(_BF16)
    w_gate, w_proj = ple_w_gate.astype(_BF16), ple_w_proj.astype(_BF16)
    shared = _stat_consts()
    cos, sin = _rope_tables(positions)

    h = x.reshape(T, D)
    for i in range(DEPTH):
        j = i // 2
        if i % 2 == 0:
            layer_ops = [(g_mix, i), (wdown, j), (gql, j), (gkvl, j), (wuq, j), (wukv, j), (gq, j),
                         (gkn, j), (gkr, j)]
            q, k, v = _mla_proj(h.reshape(B, S, D), cos, sin, layer_ops, shared)
            mix = _attention(q, k, v).reshape(T, MLA_HEADS * V_HEAD_DIM)
            wmix = (mla_wout, j)
        else:
            mix = _gmlp_front(h, [(g_mix, i), (gmlp_win, j), (ln_g, j), (ln_b, j), (gmlp_w_s, j),
                                  (bst, j)])
            wmix = (gmlp_wout, j)
        h = _post(mix, h, p_flat, i, [wmix, (g_ffn, i), (w_up, i), (w_dn, i), (g_ple, i),
                                      (w_gate, i), (w_proj, i)])
    return h.reshape(B, S, D)
```

```python
import functools
import math

import jax
import jax.numpy as jnp
from jax import lax
from jax.experimental import pallas as pl
from jax.experimental.pallas import tpu as pltpu

D_MODEL = 1024
DEPTH = 4
MLA_HEADS = 8
QK_NOPE_DIM = 128
QK_ROPE_DIM = 64
V_HEAD_DIM = 128
Q_LORA_RANK = 384
KV_LORA_RANK = 256
ROPE_BASE = 10000.0
GMLP_CHUNK = 128
GMLP_HALF = 2 * D_MODEL
GMLP_GROUPS = 8
GMLP_GROUP_DIM = GMLP_HALF // GMLP_GROUPS
D_FF = 4 * D_MODEL
PLE_DIM = 256
NORM_EPS = 1e-6

V7X_LANES = 128
V7X_VMEM_REQUEST_CAP = 56 << 20

ROPE_HALF = QK_ROPE_DIM // 2
ROPE_PAD = V7X_LANES
ROPE_PACK = V7X_LANES // ROPE_HALF
QK_PAD_DIM = QK_NOPE_DIM + ROPE_PAD
LAT_PAD_DIM = Q_LORA_RANK + KV_LORA_RANK + ROPE_PAD
V_EXT_DIM = 2 * V_HEAD_DIM

PROJ_ROWS = 1024
PROJ_SUB_ROWS = 256
ROPE_TABLE_ROWS = 1024
ATTN_Q_ROWS = 256
ATTN_SCORE_LOOKAHEAD = 1
ATTN_HEADS_PER_STEP = 4
POST_ROWS = 512
FF_CHUNK = 1024
SKEW_COPY_ROWS = 256
GMLP_ROWS = 1024
GMLP_SUB_ROWS = 256

_BF16 = jnp.bfloat16
_F32 = jnp.float32


def _vmem_limit(resident_bytes, streamed_bytes, scratch_bytes):
    need = resident_bytes + 2 * streamed_bytes + scratch_bytes
    return min(V7X_VMEM_REQUEST_CAP, need)


def _nbytes(shape, dtype):
    return math.prod(shape) * jnp.dtype(dtype).itemsize


def _layer_spec(arr, layer):
    idx = (layer,) + (0,) * (arr.ndim - 1)
    return pl.BlockSpec((None,) + arr.shape[1:], lambda *_: idx, pipeline_mode=pl.Buffered(1))


def _slab_bytes(arrs):
    return sum(_nbytes(a.shape[1:], a.dtype) for a in arrs)


def _dot(a, b):
    return jnp.dot(a, b, preferred_element_type=_F32)


def _rms_scale(x, n):
    ss = jnp.sum(x * x, axis=-1, keepdims=True)
    return lax.rsqrt(ss / n + NORM_EPS)


def _mean_sq(x, m):
    return _dot((x * x).astype(_BF16), m)


def _row_mean_sq(x, ones, n):
    sq = x * x
    acc = sq[:, :V7X_LANES]
    for c in range(V7X_LANES, x.shape[1], V7X_LANES):
        acc = acc + sq[:, c:c + V7X_LANES]
    return _dot(acc.astype(_BF16), ones) * (1.0 / n)


def _lane_tile(r, width):
    return jnp.concatenate([r] * (width // r.shape[1]), axis=1)


def _rope_table_kernel(pos_ref, invf_ref, cos_ref, sin_ref):
    ang = pos_ref[...].astype(_F32) * invf_ref[...]
    cos_packed, sin_packed = jnp.cos(ang), jnp.sin(ang)
    rows = ang.shape[0]
    low = lax.broadcasted_iota(jnp.int32, ang.shape, 1) < ROPE_HALF
    for g in range(ROPE_PACK):
        shift = (V7X_LANES - g * ROPE_HALF) % V7X_LANES
        c = jnp.where(low, pltpu.roll(cos_packed, shift, 1) if shift else cos_packed, 0.0)
        s = jnp.where(low, pltpu.roll(sin_packed, shift, 1) if shift else sin_packed, 0.0)
        out_rows = pl.ds(g, rows, stride=ROPE_PACK)
        cos_ref[out_rows, :] = c + pltpu.roll(c, ROPE_PAD // 2, 1)
        sin_ref[out_rows, :] = pltpu.roll(s, ROPE_PAD // 2, 1) - s


def _rope_tables(positions):
    B, S = positions.shape
    T = B * S
    inv_freq = ROPE_BASE ** (-(jnp.arange(0, QK_ROPE_DIM, 2, dtype=_F32) / QK_ROPE_DIM))
    invf = jnp.tile(inv_freq, ROPE_PACK).reshape(1, V7X_LANES)
    pos = jnp.repeat(positions.reshape(T // ROPE_PACK, ROPE_PACK), ROPE_HALF, axis=1)
    rows = min(ROPE_TABLE_ROWS, T // ROPE_PACK)
    out = jax.ShapeDtypeStruct((T, ROPE_PAD), _F32)
    out_spec = pl.BlockSpec((rows * ROPE_PACK, ROPE_PAD), lambda i: (i, 0))
    cos, sin = pl.pallas_call(
        _rope_table_kernel,
        grid=(T // ROPE_PACK // rows,),
        in_specs=[pl.BlockSpec((rows, V7X_LANES), lambda i: (i, 0)),
                  pl.BlockSpec((1, V7X_LANES), lambda i: (0, 0))],
        out_specs=[out_spec, out_spec],
        out_shape=[out, out],
        compiler_params=pltpu.CompilerParams(dimension_semantics=("parallel",)),
        name="rope_tables",
    )(pos, invf)
    return cos.reshape(B, S, ROPE_PAD), sin.reshape(B, S, ROPE_PAD)


def _mla_proj_rows(r0, h_ref, cos_ref, sin_ref, gmix_ref, wdown_ref, gql_ref, gkvl_ref, wuq_ref,
                   wukv_ref, gq_ref, gkn_ref, gkr_ref, ones_ref, mq_ref, q_ref, k_ref, v_ref):
    rs = slice(r0, r0 + PROJ_SUB_ROWS)
    ones = ones_ref[...]
    h = h_ref[0, rs, :]
    r_h = lax.rsqrt(_row_mean_sq(h, ones, D_MODEL) + NORM_EPS)
    lat = _dot((h * gmix_ref[...]).astype(_BF16), wdown_ref[...])
    yield

    def latent_scale(x, n):
        return r_h * lax.rsqrt(r_h * r_h * _row_mean_sq(x, ones, n) + NORM_EPS)

    c_q = lat[:, :Q_LORA_RANK]
    c_kv = lat[:, Q_LORA_RANK:Q_LORA_RANK + KV_LORA_RANK]
    k_rope = lat[:, Q_LORA_RANK + KV_LORA_RANK:]
    r_cq = latent_scale(c_q, Q_LORA_RANK)
    c_q = (c_q * _lane_tile(r_cq, Q_LORA_RANK) * gql_ref[...]).astype(_BF16)
    r_ckv = latent_scale(c_kv, KV_LORA_RANK)
    c_kv = (c_kv * _lane_tile(r_ckv, KV_LORA_RANK) * gkvl_ref[...]).astype(_BF16)

    cos = cos_ref[0, rs, :]
    sin_signed = sin_ref[0, rs, :]

    def rope(x):
        return x * cos + pltpu.roll(x, ROPE_PAD // 2, 1) * sin_signed

    r_kr = latent_scale(k_rope, QK_ROPE_DIM)
    k_rope = rope(k_rope * r_kr * gkr_ref[...]).astype(_BF16)

    q = _dot(c_q, wuq_ref[...])
    kv = _dot(c_kv, wukv_ref[...])
    yield
    gq = gq_ref[...]
    gkn = gkn_ref[...]
    for hd in range(MLA_HEADS):
        xq = q[:, hd * QK_PAD_DIM:(hd + 1) * QK_PAD_DIM]
        yq = xq * lax.rsqrt(_mean_sq(xq, mq_ref[...]) + NORM_EPS) * gq
        q_ref[0, hd, rs, :QK_NOPE_DIM] = yq[:, :QK_NOPE_DIM].astype(_BF16)
        q_ref[0, hd, rs, QK_NOPE_DIM:] = rope(yq[:, QK_NOPE_DIM:]).astype(_BF16)
        k_ref[0, hd, rs, QK_NOPE_DIM:] = k_rope
        v0 = MLA_HEADS * QK_NOPE_DIM + hd * V_HEAD_DIM
        v_ref[0, hd, rs, :] = kv[:, v0:v0 + V_HEAD_DIM].astype(_BF16)
        xk = kv[:, hd * QK_NOPE_DIM:(hd + 1) * QK_NOPE_DIM]
        yk = (xk * _rms_scale(xk, QK_NOPE_DIM) * gkn).astype(_BF16)
        k_ref[0, hd, rs, :QK_NOPE_DIM] = yk
    yield


def _mla_proj_kernel(*refs):
    tiles = [_mla_proj_rows(r0, *refs) for r0 in range(0, PROJ_ROWS, PROJ_SUB_ROWS)]
    n_stages = 3
    for step in range(n_stages + len(tiles) - 1):
        for t, tile in enumerate(tiles):
            if 0 <= step - t < n_stages:
                next(tile)


def _mla_proj(h, cos, sin, layer_ops, shared_ops):
    B, S, D = h.shape
    rows = PROJ_ROWS
    H = MLA_HEADS
    ops = layer_ops + shared_ops
    resident = _slab_bytes([a for a, _ in ops])
    streamed = (_nbytes((rows, D), _F32) + 2 * _nbytes((rows, ROPE_PAD), _F32)
                + 2 * _nbytes((H, rows, QK_PAD_DIM), _BF16) + _nbytes((H, rows, V_HEAD_DIM), _BF16))
    scratch = 4 * _nbytes((rows, H * QK_PAD_DIM), _F32)
    return pl.pallas_call(
        _mla_proj_kernel,
        grid=(B, S // rows),
        in_specs=[pl.BlockSpec((1, rows, D), lambda b, i: (b, i, 0)),
                  pl.BlockSpec((1, rows, ROPE_PAD), lambda b, i: (b, i, 0)),
                  pl.BlockSpec((1, rows, ROPE_PAD), lambda b, i: (b, i, 0))]
                 + [_layer_spec(a, l) for a, l in ops],
        out_specs=[pl.BlockSpec((1, H, rows, QK_PAD_DIM), lambda b, i: (b, 0, i, 0)),
                   pl.BlockSpec((1, H, rows, QK_PAD_DIM), lambda b, i: (b, 0, i, 0)),
                   pl.BlockSpec((1, H, rows, V_HEAD_DIM), lambda b, i: (b, 0, i, 0))],
        out_shape=[jax.ShapeDtypeStruct((B, H, S, QK_PAD_DIM), _BF16),
                   jax.ShapeDtypeStruct((B, H, S, QK_PAD_DIM), _BF16),
                   jax.ShapeDtypeStruct((B, H, S, V_HEAD_DIM), _BF16)],
        compiler_params=pltpu.CompilerParams(
            dimension_semantics=("parallel", "parallel"),
            vmem_limit_bytes=_vmem_limit(resident, streamed, scratch)),
        name="mla_proj",
    )(h, cos, sin, *[a for a, _ in ops])


def _attn_kernel(q_ref, k_ref, v_ref, o_ref, vext_ref):
    S = q_ref.shape[2]
    tq = ATTN_Q_ROWS
    for hd in range(ATTN_HEADS_PER_STEP):
        vext_ref[hd, :, :V_HEAD_DIM] = v_ref[0, hd]
        vext_ref[hd, :, V_HEAD_DIM:] = jnp.ones((S, V_EXT_DIM - V_HEAD_DIM), _BF16)
    row = lax.broadcasted_iota(jnp.int32, (tq, tq), 0)
    col = lax.broadcasted_iota(jnp.int32, (tq, tq), 1)
    diag_mask = col <= row
    nt_dims = (((1,), (1,)), ((), ()))

    def scores(j, hd):
        q0, kend = j * tq, (j + 1) * tq
        q = q_ref[0, hd, q0:kend, :]
        s_diag = lax.dot_general(q, k_ref[0, hd, q0:kend, :], nt_dims, preferred_element_type=_F32)
        s_diag = jnp.where(diag_mask, s_diag, -jnp.inf)
        s_past = None
        if j:
            s_past = lax.dot_general(q, k_ref[0, hd, :q0, :], nt_dims, preferred_element_type=_F32)
        return s_past, s_diag

    def finish(j, hd, s_past, s_diag):
        q0, kend = j * tq, (j + 1) * tq
        m = jnp.max(s_diag, axis=-1, keepdims=True)
        o = None
        if j:
            m = jnp.maximum(m, jnp.max(s_past, axis=-1, keepdims=True))
            o = _dot(jnp.exp2(s_past - m).astype(_BF16), vext_ref[hd, :q0, :])
        o_diag = _dot(jnp.exp2(s_diag - m).astype(_BF16), vext_ref[hd, q0:kend, :])
        o = o_diag if o is None else o + o_diag
        o_ref[0, q0:kend, hd * V_HEAD_DIM:(hd + 1) * V_HEAD_DIM] = (
            o[:, :V_HEAD_DIM] / o[:, V_HEAD_DIM:]).astype(_BF16)

    units = [(j, hd) for j in range(S // tq) for hd in range(ATTN_HEADS_PER_STEP)]
    ahead = ATTN_SCORE_LOOKAHEAD
    pending = [scores(*u) for u in units[:ahead]]
    for i, unit in enumerate(units):
        if i + ahead < len(units):
            pending.append(scores(*units[i + ahead]))
        finish(*unit, *pending.pop(0))


def _attention(q, k, v):
    B, H, S, _ = q.shape
    hps = ATTN_HEADS_PER_STEP
    streamed = hps * (2 * _nbytes((S, QK_PAD_DIM), _BF16) + 2 * _nbytes((S, V_HEAD_DIM), _BF16))
    scratch = 8 * _nbytes((ATTN_Q_ROWS, S), _F32) + hps * _nbytes((S, V_EXT_DIM), _BF16)
    return pl.pallas_call(
        _attn_kernel,
        grid=(B, H // hps),
        in_specs=[pl.BlockSpec((1, hps, S, QK_PAD_DIM), lambda b, h: (b, h, 0, 0)),
                  pl.BlockSpec((1, hps, S, QK_PAD_DIM), lambda b, h: (b, h, 0, 0)),
                  pl.BlockSpec((1, hps, S, V_HEAD_DIM), lambda b, h: (b, h, 0, 0))],
        out_specs=pl.BlockSpec((1, S, hps * V_HEAD_DIM), lambda b, h: (b, 0, h)),
        out_shape=jax.ShapeDtypeStruct((B, S, H * V_HEAD_DIM), _BF16),
        scratch_shapes=[pltpu.VMEM((hps, S, V_EXT_DIM), _BF16)],
        compiler_params=pltpu.CompilerParams(
            dimension_semantics=("parallel", "parallel"),
            vmem_limit_bytes=_vmem_limit(0, streamed, scratch)),
        name="mla_attention",
    )(q, k, v)


def _post_kernel(n_casts, mix_ref, h_ref, p_ref, wmix_ref, gffn_ref, wup_ref, wdn_ref, gple_ref,
                 wgate_ref, wproj_ref, *rest):
    cast_in, out_ref, cast_out = rest[:n_casts], rest[n_casts], rest[n_casts + 1:]
    for src, dst in zip(cast_in, cast_out):
        dst[...] = src[...].astype(_BF16)
    h = h_ref[...] + _dot(mix_ref[...], wmix_ref[...])
    hn = (h * _rms_scale(h, D_MODEL) * gffn_ref[...]).astype(_BF16)
    for c in range(D_FF // FF_CHUNK):
        up = _dot(hn, wup_ref[:, c * FF_CHUNK:(c + 1) * FF_CHUNK])
        act = jnp.square(jnp.maximum(up, 0.0)).astype(_BF16)
        h = h + _dot(act, wdn_ref[c * FF_CHUNK:(c + 1) * FF_CHUNK, :])
    hn = (h * _rms_scale(h, D_MODEL) * gple_ref[...]).astype(_BF16)
    gate = jax.nn.sigmoid(_dot(hn, wgate_ref[...]))
    proj = _dot(p_ref[...].astype(_BF16), wproj_ref[...])
    out_ref[...] = h + gate * proj


def _post(mix, h, p, layer, ops, casts):
    T, D = h.shape
    rows = POST_ROWS
    steps = T // rows
    kmix = mix.shape[1]
    resident = _slab_bytes([a for a, _ in ops])
    cast_rows = [w.shape[1] // steps for w, _ in casts]
    streamed = (_nbytes((rows, kmix), _BF16) + 2 * _nbytes((rows, D), _F32)
                + _nbytes((rows, PLE_DIM), _F32)
                + sum(_nbytes((r, w.shape[2]), _F32) + _nbytes((r, w.shape[2]), _BF16)
                      for r, (w, _) in zip(cast_rows, casts)))
    scratch = 2 * _nbytes((rows, FF_CHUNK), _F32) + 4 * _nbytes((rows, D), _F32)

    def cast_spec(width, chunk, l):
        return pl.BlockSpec((None, chunk, width), lambda i: (l, i, 0))

    outs = pl.pallas_call(
        functools.partial(_post_kernel, len(casts)),
        grid=(steps,),
        in_specs=[pl.BlockSpec((rows, kmix), lambda i: (i, 0)),
                  pl.BlockSpec((rows, D), lambda i: (i, 0)),
                  pl.BlockSpec((None, rows, PLE_DIM), lambda i: (layer, i, 0))]
                 + [_layer_spec(a, l) for a, l in ops]
                 + [cast_spec(w.shape[2], r, l) for r, (w, l) in zip(cast_rows, casts)],
        out_specs=[pl.BlockSpec((rows, D), lambda i: (i, 0))]
                  + [cast_spec(w.shape[2], r, 0) for r, (w, _) in zip(cast_rows, casts)],
        out_shape=[jax.ShapeDtypeStruct((T, D), _F32)]
                  + [jax.ShapeDtypeStruct((1,) + w.shape[1:], _BF16) for w, _ in casts],
        compiler_params=pltpu.CompilerParams(
            dimension_semantics=("parallel",),
            vmem_limit_bytes=_vmem_limit(resident, streamed, scratch)),
        name="post_ffn_ple",
    )(mix, h, p, *[a for a, _ in ops], *[w for w, _ in casts])
    return outs[0], list(outs[1:])


_GELU_C = 0.7978845608028654
_GELU_A = 0.044715


def _gelu_tanh(x):
    half_x = 0.5 * x
    inner = x * (_GELU_C + (_GELU_C * _GELU_A) * (x * x))
    return half_x + half_x * jnp.tanh(inner)


def _gmlp_rows(row0, ws_masked, h_ref, gmix_ref, win_ref, lng_ref, lnb_ref, bst_ref, y_ref):
    h = h_ref[row0:row0 + GMLP_SUB_ROWS, :]
    hn = (h * _rms_scale(h, D_MODEL) * gmix_ref[...]).astype(_BF16)
    v = _gelu_tanh(_dot(hn, win_ref[:, GMLP_HALF:2 * GMLP_HALF]))
    mu = jnp.mean(v, axis=-1, keepdims=True)
    vc = v - mu
    var = jnp.mean(vc * vc, axis=-1, keepdims=True)
    vn = (vc * lax.rsqrt(var + NORM_EPS) * lng_ref[...] + lnb_ref[...]).astype(_BF16)
    u = _gelu_tanh(_dot(hn, win_ref[:, :GMLP_HALF]))
    for g in range(GMLP_GROUPS):
        bias = bst_ref[:, g:g + 1]
        c0 = g * GMLP_GROUP_DIM
        for n in range(GMLP_SUB_ROWS // GMLP_CHUNK):
            r0 = n * GMLP_CHUNK
            sv = _dot(ws_masked[g], vn[r0:r0 + GMLP_CHUNK, c0:c0 + GMLP_GROUP_DIM]) + bias
            y_ref[row0 + r0:row0 + r0 + GMLP_CHUNK, c0:c0 + GMLP_GROUP_DIM] = (
                u[r0:r0 + GMLP_CHUNK, c0:c0 + GMLP_GROUP_DIM] * sv).astype(_BF16)


def _gmlp_kernel(h_ref, gmix_ref, win_ref, lng_ref, lnb_ref, ws_ref, bst_ref, y_ref):
    t_idx = lax.broadcasted_iota(jnp.int32, (GMLP_CHUNK, GMLP_CHUNK), 0)
    s_idx = lax.broadcasted_iota(jnp.int32, (GMLP_CHUNK, GMLP_CHUNK), 1)
    causal = s_idx <= t_idx
    ws_masked = [jnp.where(causal, ws_ref[g], 0.0).astype(_BF16) for g in range(GMLP_GROUPS)]
    for row0 in range(0, GMLP_ROWS, GMLP_SUB_ROWS):
        _gmlp_rows(row0, ws_masked, h_ref, gmix_ref, win_ref, lng_ref, lnb_ref, bst_ref, y_ref)


def _gmlp_front(h, ops):
    T, D = h.shape
    rows = GMLP_ROWS
    resident = _slab_bytes([a for a, _ in ops])
    streamed = _nbytes((rows, D), _F32) + _nbytes((rows, GMLP_HALF), _BF16)
    scratch = 6 * _nbytes((rows, GMLP_HALF), _F32)
    return pl.pallas_call(
        _gmlp_kernel,
        grid=(T // rows,),
        in_specs=[pl.BlockSpec((rows, D), lambda i: (i, 0))] + [_layer_spec(a, l) for a, l in ops],
        out_specs=pl.BlockSpec((rows, GMLP_HALF), lambda i: (i, 0)),
        out_shape=jax.ShapeDtypeStruct((T, GMLP_HALF), _BF16),
        compiler_params=pltpu.CompilerParams(
            dimension_semantics=("parallel",),
            vmem_limit_bytes=_vmem_limit(resident, streamed, scratch)),
        name="gmlp_front",
    )(h, *[a for a, _ in ops])


def _rope_pad(w):
    z = jnp.zeros(w.shape[:-1] + (ROPE_HALF,), w.dtype)
    return jnp.concatenate([w[..., :ROPE_HALF], z, w[..., ROPE_HALF:], z], axis=-1)


def _skew_pitch_kernel(w_ref, out_ref):
    n = w_ref.shape[-1]
    out_ref[:, :n] = w_ref[...].astype(_BF16)
    out_ref[:, n:] = jnp.zeros((out_ref.shape[0], out_ref.shape[1] - n), _BF16)


def _skew_pitch(w):
    L, K, N = w.shape
    rows = SKEW_COPY_ROWS
    return pl.pallas_call(
        _skew_pitch_kernel,
        grid=(L, K // rows),
        in_specs=[pl.BlockSpec((None, rows, N), lambda l, i: (l, i, 0))],
        out_specs=pl.BlockSpec((None, rows, N + V7X_LANES), lambda l, i: (l, i, 0)),
        out_shape=jax.ShapeDtypeStruct((L, K, N + V7X_LANES), _BF16),
        compiler_params=pltpu.CompilerParams(dimension_semantics=("parallel", "parallel")),
        name="skew_pitch",
    )(w)


def _rows(g):
    return g.reshape(g.shape[0], 1, g.shape[-1]).astype(_F32)


def _mla_params(w_down, w_uq, w_ukv, q_nope_g, q_rope_g, k_nope_g, k_rope_g):
    H = MLA_HEADS
    L = w_down.shape[0]
    n_lat = Q_LORA_RANK + KV_LORA_RANK
    wdown = jnp.concatenate([w_down[..., :n_lat], _rope_pad(w_down[..., n_lat:])], axis=-1)
    uq = w_uq.reshape(L, Q_LORA_RANK, H, QK_NOPE_DIM + QK_ROPE_DIM)
    wuq = jnp.concatenate([uq[..., :QK_NOPE_DIM], _rope_pad(uq[..., QK_NOPE_DIM:])], axis=-1)
    wuq = wuq.reshape(L, Q_LORA_RANK, H * QK_PAD_DIM)
    ukv = w_ukv.reshape(L, KV_LORA_RANK, H, QK_NOPE_DIM + V_HEAD_DIM)
    wukv = jnp.concatenate([ukv[..., :QK_NOPE_DIM].reshape(L, KV_LORA_RANK, H * QK_NOPE_DIM),
                            ukv[..., QK_NOPE_DIM:].reshape(L, KV_LORA_RANK, H * V_HEAD_DIM)], axis=-1)
    q_scale = (QK_NOPE_DIM + QK_ROPE_DIM) ** -0.5 * math.log2(math.e)
    gq = _rows(jnp.concatenate([q_nope_g, _rope_pad(q_rope_g)], axis=-1)) * q_scale
    gkn = _rows(k_nope_g)
    return (wdown.astype(_BF16), wuq.astype(_BF16), wukv.astype(_BF16), gq, gkn,
            _rows(_rope_pad(k_rope_g)))


def _group_mean_matrix(group_sizes, counts):
    blocks = []
    width = sum(group_sizes)
    off = 0
    for size, count in zip(group_sizes, counts):
        col = jnp.zeros((size, width), _F32).at[:, off:off + size].set(1.0 / count)
        blocks.append(col)
        off += size
    return jnp.concatenate(blocks, axis=0).astype(_BF16)[None]


def _stat_consts():
    ones = jnp.ones((1, V7X_LANES, V7X_LANES), _BF16)
    mq = _group_mean_matrix((QK_NOPE_DIM, ROPE_PAD), (QK_NOPE_DIM, QK_ROPE_DIM))
    return [(c, 0) for c in (ones, mq)]


def kernel(x, p, positions, norm_mix, norm_ffn, norm_ple, mla_w_down, mla_q_lora_g, mla_kv_lora_g,
           mla_w_uq, mla_w_ukv, mla_q_nope_g, mla_q_rope_g, mla_k_nope_g, mla_k_rope_g, mla_w_out,
           gmlp_w_in, gmlp_ln_g, gmlp_ln_b, gmlp_w_s, gmlp_b_s, gmlp_w_out, ffn_w_up, ffn_w_down,
           ple_w_gate, ple_w_proj):
    B, S, D = x.shape
    T = B * S
    p_flat = p.reshape(DEPTH, T, PLE_DIM)
    g_mix, g_ffn, g_ple = _rows(norm_mix), _rows(norm_ffn), _rows(norm_ple)
    wdown, wuq, wukv, gq, gkn, gkr = _mla_params(mla_w_down, mla_w_uq, mla_w_ukv, mla_q_nope_g,
                                                 mla_q_rope_g, mla_k_nope_g, mla_k_rope_g)
    gql, gkvl = _rows(mla_q_lora_g), _rows(mla_kv_lora_g)
    gmlp_win = _skew_pitch(gmlp_w_in)
    ln_g, ln_b = _rows(gmlp_ln_g), _rows(gmlp_ln_b)
    bst = jnp.swapaxes(gmlp_b_s, 1, 2)
    w_proj = ple_w_proj.astype(_BF16)
    shared = _stat_consts()
    cos, sin = _rope_tables(positions)

    def post_weights_f32(i):
        mix_w = (mla_w_out, i // 2) if i % 2 == 0 else (gmlp_w_out, i // 2)
        return [mix_w, (ffn_w_up, i), (ffn_w_down, i), (ple_w_gate, i)]

    post_w = [w[l:l + 1].astype(_BF16) for w, l in post_weights_f32(0)]

    h = x.reshape(T, D)
    for i in range(DEPTH):
        j = i // 2
        if i % 2 == 0:
            layer_ops = [(g_mix, i), (wdown, j), (gql, j), (gkvl, j), (wuq, j), (wukv, j), (gq, j),
                         (gkn, j), (gkr, j)]
            q, k, v = _mla_proj(h.reshape(B, S, D), cos, sin, layer_ops, shared)
            mix = _attention(q, k, v).reshape(T, MLA_HEADS * V_HEAD_DIM)
        else:
            mix = _gmlp_front(h, [(g_mix, i), (gmlp_win, j), (ln_g, j), (ln_b, j), (gmlp_w_s, j),
                                  (bst, j)])
        wmix, w_up, w_dn, w_gate = post_w
        casts = post_weights_f32(i + 1) if i + 1 < DEPTH else []
        h, post_w = _post(mix, h, p_flat, i, [(wmix, 0), (g_ffn, i), (w_up, 0), (w_dn, 0),
                                              (g_ple, i), (w_gate, 0), (w_proj, i)], casts)
    return h.reshape(B, S, D)
```

```python
import functools
import math

import jax
import jax.numpy as jnp
from jax import lax
from jax.experimental import pallas as pl
from jax.experimental.pallas import tpu as pltpu

D_MODEL = 1024
DEPTH = 4
MLA_HEADS = 8
QK_NOPE_DIM = 128
QK_ROPE_DIM = 64
V_HEAD_DIM = 128
Q_LORA_RANK = 384
KV_LORA_RANK = 256
ROPE_BASE = 10000.0
GMLP_CHUNK = 128
GMLP_HALF = 2 * D_MODEL
GMLP_GROUPS = 8
GMLP_GROUP_DIM = GMLP_HALF // GMLP_GROUPS
D_FF = 4 * D_MODEL
PLE_DIM = 256
NORM_EPS = 1e-6

V7X_LANES = 128
V7X_VMEM_REQUEST_CAP = 56 << 20

ROPE_HALF = QK_ROPE_DIM // 2
ROPE_PAD = V7X_LANES
ROPE_PACK = V7X_LANES // ROPE_HALF
QK_PAD_DIM = QK_NOPE_DIM + ROPE_PAD
LAT_PAD_DIM = Q_LORA_RANK + KV_LORA_RANK + ROPE_PAD
V_EXT_DIM = 2 * V_HEAD_DIM

PROJ_ROWS = 1024
PROJ_SUB_ROWS = 256
ROPE_TABLE_ROWS = 1024
ATTN_Q_ROWS = 256
ATTN_SCORE_LOOKAHEAD = 1
ATTN_HEADS_PER_STEP = 4
POST_ROWS = 512
FF_CHUNK = 1024
SKEW_COPY_ROWS = 256
CAST_STEPS = 8
GMLP_ROWS = 1024
GMLP_SUB_ROWS = 256

_BF16 = jnp.bfloat16
_F32 = jnp.float32


def _vmem_limit(resident_bytes, streamed_bytes, scratch_bytes):
    need = resident_bytes + 2 * streamed_bytes + scratch_bytes
    return min(V7X_VMEM_REQUEST_CAP, need)


def _nbytes(shape, dtype):
    return math.prod(shape) * jnp.dtype(dtype).itemsize


def _layer_spec(arr, layer):
    idx = (layer,) + (0,) * (arr.ndim - 1)
    return pl.BlockSpec((None,) + arr.shape[1:], lambda *_: idx, pipeline_mode=pl.Buffered(1))


def _slab_bytes(arrs):
    return sum(_nbytes(a.shape[1:], a.dtype) for a in arrs)


def _dot(a, b):
    return jnp.dot(a, b, preferred_element_type=_F32)


def _rms_scale(x, n):
    ss = jnp.sum(x * x, axis=-1, keepdims=True)
    return lax.rsqrt(ss / n + NORM_EPS)


def _mean_sq(x, m):
    return _dot((x * x).astype(_BF16), m)


def _row_mean_sq(x, ones, n):
    sq = x * x
    acc = sq[:, :V7X_LANES]
    for c in range(V7X_LANES, x.shape[1], V7X_LANES):
        acc = acc + sq[:, c:c + V7X_LANES]
    return _dot(acc.astype(_BF16), ones) * (1.0 / n)


def _lane_tile(r, width):
    return jnp.concatenate([r] * (width // r.shape[1]), axis=1)


def _rope_table_kernel(pos_ref, invf_ref, cos_ref, sin_ref):
    ang = pos_ref[...].astype(_F32) * invf_ref[...]
    cos_packed, sin_packed = jnp.cos(ang), jnp.sin(ang)
    rows = ang.shape[0]
    low = lax.broadcasted_iota(jnp.int32, ang.shape, 1) < ROPE_HALF
    for g in range(ROPE_PACK):
        shift = (V7X_LANES - g * ROPE_HALF) % V7X_LANES
        c = jnp.where(low, pltpu.roll(cos_packed, shift, 1) if shift else cos_packed, 0.0)
        s = jnp.where(low, pltpu.roll(sin_packed, shift, 1) if shift else sin_packed, 0.0)
        out_rows = pl.ds(g, rows, stride=ROPE_PACK)
        cos_ref[out_rows, :] = c + pltpu.roll(c, ROPE_PAD // 2, 1)
        sin_ref[out_rows, :] = pltpu.roll(s, ROPE_PAD // 2, 1) - s


def _rope_tables(positions):
    B, S = positions.shape
    T = B * S
    inv_freq = ROPE_BASE ** (-(jnp.arange(0, QK_ROPE_DIM, 2, dtype=_F32) / QK_ROPE_DIM))
    invf = jnp.tile(inv_freq, ROPE_PACK).reshape(1, V7X_LANES)
    pos = jnp.repeat(positions.reshape(T // ROPE_PACK, ROPE_PACK), ROPE_HALF, axis=1)
    rows = min(ROPE_TABLE_ROWS, T // ROPE_PACK)
    out = jax.ShapeDtypeStruct((T, ROPE_PAD), _F32)
    out_spec = pl.BlockSpec((rows * ROPE_PACK, ROPE_PAD), lambda i: (i, 0))
    cos, sin = pl.pallas_call(
        _rope_table_kernel,
        grid=(T // ROPE_PACK // rows,),
        in_specs=[pl.BlockSpec((rows, V7X_LANES), lambda i: (i, 0)),
                  pl.BlockSpec((1, V7X_LANES), lambda i: (0, 0))],
        out_specs=[out_spec, out_spec],
        out_shape=[out, out],
        compiler_params=pltpu.CompilerParams(dimension_semantics=("parallel",)),
        name="rope_tables",
    )(pos, invf)
    return cos.reshape(B, S, ROPE_PAD), sin.reshape(B, S, ROPE_PAD)


def _mla_proj_rows(r0, h_ref, cos_ref, sin_ref, gmix_ref, wdown_ref, gql_ref, gkvl_ref, wuq_ref,
                   wukv_ref, gq_ref, gkn_ref, gkr_ref, ones_ref, mq_ref, q_ref, k_ref, v_ref):
    rs = slice(r0, r0 + PROJ_SUB_ROWS)
    ones = ones_ref[...]
    h = h_ref[0, rs, :]
    r_h = lax.rsqrt(_row_mean_sq(h, ones, D_MODEL) + NORM_EPS)
    lat = _dot((h * gmix_ref[...]).astype(_BF16), wdown_ref[...])
    yield

    def latent_scale(x, n):
        return r_h * lax.rsqrt(r_h * r_h * _row_mean_sq(x, ones, n) + NORM_EPS)

    c_q = lat[:, :Q_LORA_RANK]
    c_kv = lat[:, Q_LORA_RANK:Q_LORA_RANK + KV_LORA_RANK]
    k_rope = lat[:, Q_LORA_RANK + KV_LORA_RANK:]
    r_cq = latent_scale(c_q, Q_LORA_RANK)
    c_q = (c_q * _lane_tile(r_cq, Q_LORA_RANK) * gql_ref[...]).astype(_BF16)
    r_ckv = latent_scale(c_kv, KV_LORA_RANK)
    c_kv = (c_kv * _lane_tile(r_ckv, KV_LORA_RANK) * gkvl_ref[...]).astype(_BF16)

    cos = cos_ref[0, rs, :]
    sin_signed = sin_ref[0, rs, :]

    def rope(x):
        return x * cos + pltpu.roll(x, ROPE_PAD // 2, 1) * sin_signed

    r_kr = latent_scale(k_rope, QK_ROPE_DIM)
    k_rope = rope(k_rope * r_kr * gkr_ref[...]).astype(_BF16)

    q = _dot(c_q, wuq_ref[...])
    kv = _dot(c_kv, wukv_ref[...])
    yield
    gq = gq_ref[...]
    gkn = gkn_ref[...]
    for hd in range(MLA_HEADS):
        xq = q[:, hd * QK_PAD_DIM:(hd + 1) * QK_PAD_DIM]
        yq = xq * lax.rsqrt(_mean_sq(xq, mq_ref[...]) + NORM_EPS) * gq
        q_ref[0, hd, rs, :QK_NOPE_DIM] = yq[:, :QK_NOPE_DIM].astype(_BF16)
        q_ref[0, hd, rs, QK_NOPE_DIM:] = rope(yq[:, QK_NOPE_DIM:]).astype(_BF16)
        k_ref[0, hd, rs, QK_NOPE_DIM:] = k_rope
        v0 = MLA_HEADS * QK_NOPE_DIM + hd * V_HEAD_DIM
        v_ref[0, hd, rs, :] = kv[:, v0:v0 + V_HEAD_DIM].astype(_BF16)
        xk = kv[:, hd * QK_NOPE_DIM:(hd + 1) * QK_NOPE_DIM]
        yk = (xk * _rms_scale(xk, QK_NOPE_DIM) * gkn).astype(_BF16)
        k_ref[0, hd, rs, :QK_NOPE_DIM] = yk
    yield


def _mla_proj_kernel(*refs):
    tiles = [_mla_proj_rows(r0, *refs) for r0 in range(0, PROJ_ROWS, PROJ_SUB_ROWS)]
    n_stages = 3
    for step in range(n_stages + len(tiles) - 1):
        for t, tile in enumerate(tiles):
            if 0 <= step - t < n_stages:
                next(tile)


def _mla_proj(h, cos, sin, layer_ops, shared_ops):
    B, S, D = h.shape
    rows = PROJ_ROWS
    H = MLA_HEADS
    ops = layer_ops + shared_ops
    resident = _slab_bytes([a for a, _ in ops])
    streamed = (_nbytes((rows, D), _F32) + 2 * _nbytes((rows, ROPE_PAD), _F32)
                + 2 * _nbytes((H, rows, QK_PAD_DIM), _BF16) + _nbytes((H, rows, V_HEAD_DIM), _BF16))
    scratch = 4 * _nbytes((rows, H * QK_PAD_DIM), _F32)
    return pl.pallas_call(
        _mla_proj_kernel,
        grid=(B, S // rows),
        in_specs=[pl.BlockSpec((1, rows, D), lambda b, i: (b, i, 0)),
                  pl.BlockSpec((1, rows, ROPE_PAD), lambda b, i: (b, i, 0)),
                  pl.BlockSpec((1, rows, ROPE_PAD), lambda b, i: (b, i, 0))]
                 + [_layer_spec(a, l) for a, l in ops],
        out_specs=[pl.BlockSpec((1, H, rows, QK_PAD_DIM), lambda b, i: (b, 0, i, 0)),
                   pl.BlockSpec((1, H, rows, QK_PAD_DIM), lambda b, i: (b, 0, i, 0)),
                   pl.BlockSpec((1, H, rows, V_HEAD_DIM), lambda b, i: (b, 0, i, 0))],
        out_shape=[jax.ShapeDtypeStruct((B, H, S, QK_PAD_DIM), _BF16),
                   jax.ShapeDtypeStruct((B, H, S, QK_PAD_DIM), _BF16),
                   jax.ShapeDtypeStruct((B, H, S, V_HEAD_DIM), _BF16)],
        compiler_params=pltpu.CompilerParams(
            dimension_semantics=("parallel", "parallel"),
            vmem_limit_bytes=_vmem_limit(resident, streamed, scratch)),
        name="mla_proj",
    )(h, cos, sin, *[a for a, _ in ops])


def _attn_kernel(q_ref, k_ref, v_ref, o_ref, vext_ref):
    S = q_ref.shape[2]
    tq = ATTN_Q_ROWS
    for hd in range(ATTN_HEADS_PER_STEP):
        vext_ref[hd, :, :V_HEAD_DIM] = v_ref[0, hd]
        vext_ref[hd, :, V_HEAD_DIM:] = jnp.ones((S, V_EXT_DIM - V_HEAD_DIM), _BF16)
    row = lax.broadcasted_iota(jnp.int32, (tq, tq), 0)
    col = lax.broadcasted_iota(jnp.int32, (tq, tq), 1)
    diag_mask = col <= row
    nt_dims = (((1,), (1,)), ((), ()))

    def scores(j, hd):
        q0, kend = j * tq, (j + 1) * tq
        q = q_ref[0, hd, q0:kend, :]
        s_diag = lax.dot_general(q, k_ref[0, hd, q0:kend, :], nt_dims, preferred_element_type=_F32)
        s_diag = jnp.where(diag_mask, s_diag, -jnp.inf)
        s_past = None
        if j:
            s_past = lax.dot_general(q, k_ref[0, hd, :q0, :], nt_dims, preferred_element_type=_F32)
        return s_past, s_diag

    def finish(j, hd, s_past, s_diag):
        q0, kend = j * tq, (j + 1) * tq
        m = jnp.max(s_diag, axis=-1, keepdims=True)
        o = None
        if j:
            m = jnp.maximum(m, jnp.max(s_past, axis=-1, keepdims=True))
            o = _dot(jnp.exp2(s_past - m).astype(_BF16), vext_ref[hd, :q0, :])
        o_diag = _dot(jnp.exp2(s_diag - m).astype(_BF16), vext_ref[hd, q0:kend, :])
        o = o_diag if o is None else o + o_diag
        o_ref[0, q0:kend, hd * V_HEAD_DIM:(hd + 1) * V_HEAD_DIM] = (
            o[:, :V_HEAD_DIM] / o[:, V_HEAD_DIM:]).astype(_BF16)

    units = [(j, hd) for j in range(S // tq) for hd in range(ATTN_HEADS_PER_STEP)]
    ahead = ATTN_SCORE_LOOKAHEAD
    pending = [scores(*u) for u in units[:ahead]]
    for i, unit in enumerate(units):
        if i + ahead < len(units):
            pending.append(scores(*units[i + ahead]))
        finish(*unit, *pending.pop(0))


def _attention(q, k, v):
    B, H, S, _ = q.shape
    hps = ATTN_HEADS_PER_STEP
    streamed = hps * (2 * _nbytes((S, QK_PAD_DIM), _BF16) + 2 * _nbytes((S, V_HEAD_DIM), _BF16))
    scratch = 8 * _nbytes((ATTN_Q_ROWS, S), _F32) + hps * _nbytes((S, V_EXT_DIM), _BF16)
    return pl.pallas_call(
        _attn_kernel,
        grid=(B, H // hps),
        in_specs=[pl.BlockSpec((1, hps, S, QK_PAD_DIM), lambda b, h: (b, h, 0, 0)),
                  pl.BlockSpec((1, hps, S, QK_PAD_DIM), lambda b, h: (b, h, 0, 0)),
                  pl.BlockSpec((1, hps, S, V_HEAD_DIM), lambda b, h: (b, h, 0, 0))],
        out_specs=pl.BlockSpec((1, S, hps * V_HEAD_DIM), lambda b, h: (b, 0, h)),
        out_shape=jax.ShapeDtypeStruct((B, S, H * V_HEAD_DIM), _BF16),
        scratch_shapes=[pltpu.VMEM((hps, S, V_EXT_DIM), _BF16)],
        compiler_params=pltpu.CompilerParams(
            dimension_semantics=("parallel", "parallel"),
            vmem_limit_bytes=_vmem_limit(0, streamed, scratch)),
        name="mla_attention",
    )(q, k, v)


def _post_kernel(n_casts, mix_ref, h_ref, p_ref, wmix_ref, gffn_ref, wup_ref, wdn_ref, gple_ref,
                 wgate_ref, wproj_ref, *rest):
    cast_in, out_ref, cast_out = rest[:n_casts], rest[n_casts], rest[n_casts + 1:]
    for src, dst in zip(cast_in, cast_out):
        dst[...] = src[...].astype(_BF16)
    h = h_ref[...] + _dot(mix_ref[...], wmix_ref[...])
    hn = (h * _rms_scale(h, D_MODEL) * gffn_ref[...]).astype(_BF16)
    for c in range(D_FF // FF_CHUNK):
        up = _dot(hn, wup_ref[:, c * FF_CHUNK:(c + 1) * FF_CHUNK])
        act = jnp.square(jnp.maximum(up, 0.0)).astype(_BF16)
        h = h + _dot(act, wdn_ref[c * FF_CHUNK:(c + 1) * FF_CHUNK, :])
    hn = (h * _rms_scale(h, D_MODEL) * gple_ref[...]).astype(_BF16)
    gate = jax.nn.sigmoid(_dot(hn, wgate_ref[...]))
    proj = _dot(p_ref[...].astype(_BF16), wproj_ref[...])
    out_ref[...] = h + gate * proj


def _post(mix, h, p, layer, ops, casts):
    T, D = h.shape
    rows = POST_ROWS
    steps = T // rows
    kmix = mix.shape[1]
    resident = _slab_bytes([a for a, _ in ops])
    cast_rows = [w.shape[1] // steps for w, _ in casts]
    streamed = (_nbytes((rows, kmix), _BF16) + 2 * _nbytes((rows, D), _F32)
                + _nbytes((rows, PLE_DIM), _F32)
                + sum(_nbytes((r, w.shape[2]), _F32) + _nbytes((r, w.shape[2]), _BF16)
                      for r, (w, _) in zip(cast_rows, casts)))
    scratch = 2 * _nbytes((rows, FF_CHUNK), _F32) + 4 * _nbytes((rows, D), _F32)

    def cast_spec(width, chunk, l):
        return pl.BlockSpec((None, chunk, width), lambda i: (l, i, 0))

    outs = pl.pallas_call(
        functools.partial(_post_kernel, len(casts)),
        grid=(steps,),
        in_specs=[pl.BlockSpec((rows, kmix), lambda i: (i, 0)),
                  pl.BlockSpec((rows, D), lambda i: (i, 0)),
                  pl.BlockSpec((None, rows, PLE_DIM), lambda i: (layer, i, 0))]
                 + [_layer_spec(a, l) for a, l in ops]
                 + [cast_spec(w.shape[2], r, l) for r, (w, l) in zip(cast_rows, casts)],
        out_specs=[pl.BlockSpec((rows, D), lambda i: (i, 0))]
                  + [cast_spec(w.shape[2], r, 0) for r, (w, _) in zip(cast_rows, casts)],
        out_shape=[jax.ShapeDtypeStruct((T, D), _F32)]
                  + [jax.ShapeDtypeStruct((1,) + w.shape[1:], _BF16) for w, _ in casts],
        compiler_params=pltpu.CompilerParams(
            dimension_semantics=("parallel",),
            vmem_limit_bytes=_vmem_limit(resident, streamed, scratch)),
        name="post_ffn_ple",
    )(mix, h, p, *[a for a, _ in ops], *[w for w, _ in casts])
    return outs[0], list(outs[1:])


_GELU_C = 0.7978845608028654
_GELU_A = 0.044715


def _gelu_tanh(x):
    half_x = 0.5 * x
    inner = x * (_GELU_C + (_GELU_C * _GELU_A) * (x * x))
    return half_x + half_x * jnp.tanh(inner)


def _gmlp_rows(row0, ws_masked, h_ref, gmix_ref, win_ref, lng_ref, lnb_ref, bst_ref, y_ref):
    h = h_ref[row0:row0 + GMLP_SUB_ROWS, :]
    hn = (h * _rms_scale(h, D_MODEL) * gmix_ref[...]).astype(_BF16)
    v = _gelu_tanh(_dot(hn, win_ref[:, GMLP_HALF:2 * GMLP_HALF]))
    mu = jnp.mean(v, axis=-1, keepdims=True)
    vc = v - mu
    var = jnp.mean(vc * vc, axis=-1, keepdims=True)
    vn = (vc * lax.rsqrt(var + NORM_EPS) * lng_ref[...] + lnb_ref[...]).astype(_BF16)
    u = _gelu_tanh(_dot(hn, win_ref[:, :GMLP_HALF]))
    for g in range(GMLP_GROUPS):
        bias = bst_ref[:, g:g + 1]
        c0 = g * GMLP_GROUP_DIM
        for n in range(GMLP_SUB_ROWS // GMLP_CHUNK):
            r0 = n * GMLP_CHUNK
            sv = _dot(ws_masked[g], vn[r0:r0 + GMLP_CHUNK, c0:c0 + GMLP_GROUP_DIM]) + bias
            y_ref[row0 + r0:row0 + r0 + GMLP_CHUNK, c0:c0 + GMLP_GROUP_DIM] = (
                u[r0:r0 + GMLP_CHUNK, c0:c0 + GMLP_GROUP_DIM] * sv).astype(_BF16)


def _gmlp_kernel(h_ref, gmix_ref, win_ref, lng_ref, lnb_ref, ws_ref, bst_ref, y_ref):
    t_idx = lax.broadcasted_iota(jnp.int32, (GMLP_CHUNK, GMLP_CHUNK), 0)
    s_idx = lax.broadcasted_iota(jnp.int32, (GMLP_CHUNK, GMLP_CHUNK), 1)
    causal = s_idx <= t_idx
    ws_masked = [jnp.where(causal, ws_ref[g], 0.0).astype(_BF16) for g in range(GMLP_GROUPS)]
    for row0 in range(0, GMLP_ROWS, GMLP_SUB_ROWS):
        _gmlp_rows(row0, ws_masked, h_ref, gmix_ref, win_ref, lng_ref, lnb_ref, bst_ref, y_ref)


def _gmlp_front(h, ops):
    T, D = h.shape
    rows = GMLP_ROWS
    resident = _slab_bytes([a for a, _ in ops])
    streamed = _nbytes((rows, D), _F32) + _nbytes((rows, GMLP_HALF), _BF16)
    scratch = 6 * _nbytes((rows, GMLP_HALF), _F32)
    return pl.pallas_call(
        _gmlp_kernel,
        grid=(T // rows,),
        in_specs=[pl.BlockSpec((rows, D), lambda i: (i, 0))] + [_layer_spec(a, l) for a, l in ops],
        out_specs=pl.BlockSpec((rows, GMLP_HALF), lambda i: (i, 0)),
        out_shape=jax.ShapeDtypeStruct((T, GMLP_HALF), _BF16),
        compiler_params=pltpu.CompilerParams(
            dimension_semantics=("parallel",),
            vmem_limit_bytes=_vmem_limit(resident, streamed, scratch)),
        name="gmlp_front",
    )(h, *[a for a, _ in ops])


def _rope_pad(w):
    z = jnp.zeros(w.shape[:-1] + (ROPE_HALF,), w.dtype)
    return jnp.concatenate([w[..., :ROPE_HALF], z, w[..., ROPE_HALF:], z], axis=-1)


def _skew_pitch_kernel(w_ref, out_ref):
    n = w_ref.shape[-1]
    out_ref[:, :n] = w_ref[...].astype(_BF16)
    out_ref[:, n:] = jnp.zeros((out_ref.shape[0], out_ref.shape[1] - n), _BF16)


def _skew_pitch(w):
    L, K, N = w.shape
    rows = SKEW_COPY_ROWS
    return pl.pallas_call(
        _skew_pitch_kernel,
        grid=(L, K // rows),
        in_specs=[pl.BlockSpec((None, rows, N), lambda l, i: (l, i, 0))],
        out_specs=pl.BlockSpec((None, rows, N + V7X_LANES), lambda l, i: (l, i, 0)),
        out_shape=jax.ShapeDtypeStruct((L, K, N + V7X_LANES), _BF16),
        compiler_params=pltpu.CompilerParams(dimension_semantics=("parallel", "parallel")),
        name="skew_pitch",
    )(w)


def _cast_layers_kernel(*refs):
    n = len(refs) // 2
    for src, dst in zip(refs[:n], refs[n:]):
        dst[...] = src[...].astype(_BF16)


def _cast_layers(weights):
    steps = CAST_STEPS

    def spec(w, l):
        return pl.BlockSpec((None, w.shape[1] // steps, w.shape[2]), lambda i: (l, i, 0))

    return pl.pallas_call(
        _cast_layers_kernel,
        grid=(steps,),
        in_specs=[spec(w, l) for w, l in weights],
        out_specs=[spec(w, 0) for w, _ in weights],
        out_shape=[jax.ShapeDtypeStruct((1,) + w.shape[1:], _BF16) for w, _ in weights],
        compiler_params=pltpu.CompilerParams(dimension_semantics=("parallel",)),
        name="cast_layers",
    )(*[w for w, _ in weights])


def _rows(g):
    return g.reshape(g.shape[0], 1, g.shape[-1]).astype(_F32)


def _mla_params(w_down, w_uq, w_ukv, q_nope_g, q_rope_g, k_nope_g, k_rope_g):
    H = MLA_HEADS
    L = w_down.shape[0]
    n_lat = Q_LORA_RANK + KV_LORA_RANK
    wdown = jnp.concatenate([w_down[..., :n_lat], _rope_pad(w_down[..., n_lat:])], axis=-1)
    uq = w_uq.reshape(L, Q_LORA_RANK, H, QK_NOPE_DIM + QK_ROPE_DIM)
    wuq = jnp.concatenate([uq[..., :QK_NOPE_DIM], _rope_pad(uq[..., QK_NOPE_DIM:])], axis=-1)
    wuq = wuq.reshape(L, Q_LORA_RANK, H * QK_PAD_DIM)
    ukv = w_ukv.reshape(L, KV_LORA_RANK, H, QK_NOPE_DIM + V_HEAD_DIM)
    wukv = jnp.concatenate([ukv[..., :QK_NOPE_DIM].reshape(L, KV_LORA_RANK, H * QK_NOPE_DIM),
                            ukv[..., QK_NOPE_DIM:].reshape(L, KV_LORA_RANK, H * V_HEAD_DIM)], axis=-1)
    q_scale = (QK_NOPE_DIM + QK_ROPE_DIM) ** -0.5 * math.log2(math.e)
    gq = _rows(jnp.concatenate([q_nope_g, _rope_pad(q_rope_g)], axis=-1)) * q_scale
    gkn = _rows(k_nope_g)
    return (wdown.astype(_BF16), wuq.astype(_BF16), wukv.astype(_BF16), gq, gkn,
            _rows(_rope_pad(k_rope_g)))


def _group_mean_matrix(group_sizes, counts):
    blocks = []
    width = sum(group_sizes)
    off = 0
    for size, count in zip(group_sizes, counts):
        col = jnp.zeros((size, width), _F32).at[:, off:off + size].set(1.0 / count)
        blocks.append(col)
        off += size
    return jnp.concatenate(blocks, axis=0).astype(_BF16)[None]


def _stat_consts():
    ones = jnp.ones((1, V7X_LANES, V7X_LANES), _BF16)
    mq = _group_mean_matrix((QK_NOPE_DIM, ROPE_PAD), (QK_NOPE_DIM, QK_ROPE_DIM))
    return [(c, 0) for c in (ones, mq)]


def kernel(x, p, positions, norm_mix, norm_ffn, norm_ple, mla_w_down, mla_q_lora_g, mla_kv_lora_g,
           mla_w_uq, mla_w_ukv, mla_q_nope_g, mla_q_rope_g, mla_k_nope_g, mla_k_rope_g, mla_w_out,
           gmlp_w_in, gmlp_ln_g, gmlp_ln_b, gmlp_w_s, gmlp_b_s, gmlp_w_out, ffn_w_up, ffn_w_down,
           ple_w_gate, ple_w_proj):
    B, S, D = x.shape
    T = B * S
    p_flat = p.reshape(DEPTH, T, PLE_DIM)
    g_mix, g_ffn, g_ple = _rows(norm_mix), _rows(norm_ffn), _rows(norm_ple)
    wdown, wuq, wukv, gq, gkn, gkr = _mla_params(mla_w_down, mla_w_uq, mla_w_ukv, mla_q_nope_g,
                                                 mla_q_rope_g, mla_k_nope_g, mla_k_rope_g)
    gql, gkvl = _rows(mla_q_lora_g), _rows(mla_kv_lora_g)
    gmlp_win = _skew_pitch(gmlp_w_in)
    ln_g, ln_b = _rows(gmlp_ln_g), _rows(gmlp_ln_b)
    bst = jnp.swapaxes(gmlp_b_s, 1, 2)
    w_proj = ple_w_proj.astype(_BF16)
    shared = _stat_consts()
    cos, sin = _rope_tables(positions)

    def post_weights_f32(i):
        mix_w = (mla_w_out, i // 2) if i % 2 == 0 else (gmlp_w_out, i // 2)
        return [mix_w, (ffn_w_up, i), (ffn_w_down, i), (ple_w_gate, i)]

    post_w = _cast_layers(post_weights_f32(0))

    h = x.reshape(T, D)
    for i in range(DEPTH):
        j = i // 2
        if i % 2 == 0:
            layer_ops = [(g_mix, i), (wdown, j), (gql, j), (gkvl, j), (wuq, j), (wukv, j), (gq, j),
                         (gkn, j), (gkr, j)]
            q, k, v = _mla_proj(h.reshape(B, S, D), cos, sin, layer_ops, shared)
            mix = _attention(q, k, v).reshape(T, MLA_HEADS * V_HEAD_DIM)
        else:
            mix = _gmlp_front(h, [(g_mix, i), (gmlp_win, j), (ln_g, j), (ln_b, j), (gmlp_w_s, j),
                                  (bst, j)])
        wmix, w_up, w_dn, w_gate = post_w
        casts = post_weights_f32(i + 1) if i + 1 < DEPTH else []
        h, post_w = _post(mix, h, p_flat, i, [(wmix, 0), (g_ffn, i), (w_up, 0), (w_dn, 0),
                                              (g_ple, i), (w_gate, 0), (w_proj, i)], casts)
    return h.reshape(B, S, D)
```

```python
import functools
import math

import jax
import jax.numpy as jnp
from jax import lax
from jax.experimental import pallas as pl
from jax.experimental.pallas import tpu as pltpu

D_MODEL = 1024
DEPTH = 4
MLA_HEADS = 8
QK_NOPE_DIM = 128
QK_ROPE_DIM = 64
V_HEAD_DIM = 128
Q_LORA_RANK = 384
KV_LORA_RANK = 256
ROPE_BASE = 10000.0
GMLP_CHUNK = 128
GMLP_HALF = 2 * D_MODEL
GMLP_GROUPS = 8
GMLP_GROUP_DIM = GMLP_HALF // GMLP_GROUPS
D_FF = 4 * D_MODEL
PLE_DIM = 256
NORM_EPS = 1e-6

V7X_LANES = 128
V7X_VMEM_REQUEST_CAP = 56 << 20

ROPE_HALF = QK_ROPE_DIM // 2
ROPE_PAD = V7X_LANES
ROPE_PACK = V7X_LANES // ROPE_HALF
QK_PAD_DIM = QK_NOPE_DIM + ROPE_PAD
LAT_PAD_DIM = Q_LORA_RANK + KV_LORA_RANK + ROPE_PAD
V_EXT_DIM = 2 * V_HEAD_DIM

PROJ_ROWS = 1024
PROJ_SUB_ROWS = 256
ROPE_TABLE_ROWS = 1024
ATTN_Q_ROWS = 256
ATTN_SCORE_LOOKAHEAD = 1
ATTN_HEADS_PER_STEP = 4
POST_ROWS = 512
FF_CHUNK = 1024
SKEW_COPY_ROWS = 256
GMLP_ROWS = 1024
GMLP_SUB_ROWS = 256
GMLP_LAYER_ROWS = 512

_BF16 = jnp.bfloat16
_F32 = jnp.float32


def _vmem_limit(resident_bytes, streamed_bytes, scratch_bytes):
    need = resident_bytes + 2 * streamed_bytes + scratch_bytes
    return min(V7X_VMEM_REQUEST_CAP, need)


def _nbytes(shape, dtype):
    return math.prod(shape) * jnp.dtype(dtype).itemsize


def _layer_spec(arr, layer):
    idx = (layer,) + (0,) * (arr.ndim - 1)
    return pl.BlockSpec((None,) + arr.shape[1:], lambda *_: idx, pipeline_mode=pl.Buffered(1))


def _slab_bytes(arrs):
    return sum(_nbytes(a.shape[1:], a.dtype) for a in arrs)


def _dot(a, b):
    return jnp.dot(a, b, preferred_element_type=_F32)


def _rms_scale(x, n):
    ss = jnp.sum(x * x, axis=-1, keepdims=True)
    return lax.rsqrt(ss / n + NORM_EPS)


def _mean_sq(x, m):
    return _dot((x * x).astype(_BF16), m)


def _row_mean_sq(x, ones, n):
    sq = x * x
    acc = sq[:, :V7X_LANES]
    for c in range(V7X_LANES, x.shape[1], V7X_LANES):
        acc = acc + sq[:, c:c + V7X_LANES]
    return _dot(acc.astype(_BF16), ones) * (1.0 / n)


def _lane_tile(r, width):
    return jnp.concatenate([r] * (width // r.shape[1]), axis=1)


def _rope_table_kernel(pos_ref, invf_ref, cos_ref, sin_ref):
    ang = pos_ref[...].astype(_F32) * invf_ref[...]
    cos_packed, sin_packed = jnp.cos(ang), jnp.sin(ang)
    rows = ang.shape[0]
    low = lax.broadcasted_iota(jnp.int32, ang.shape, 1) < ROPE_HALF
    for g in range(ROPE_PACK):
        shift = (V7X_LANES - g * ROPE_HALF) % V7X_LANES
        c = jnp.where(low, pltpu.roll(cos_packed, shift, 1) if shift else cos_packed, 0.0)
        s = jnp.where(low, pltpu.roll(sin_packed, shift, 1) if shift else sin_packed, 0.0)
        out_rows = pl.ds(g, rows, stride=ROPE_PACK)
        cos_ref[out_rows, :] = c + pltpu.roll(c, ROPE_PAD // 2, 1)
        sin_ref[out_rows, :] = pltpu.roll(s, ROPE_PAD // 2, 1) - s


def _rope_tables(positions):
    B, S = positions.shape
    T = B * S
    inv_freq = ROPE_BASE ** (-(jnp.arange(0, QK_ROPE_DIM, 2, dtype=_F32) / QK_ROPE_DIM))
    invf = jnp.tile(inv_freq, ROPE_PACK).reshape(1, V7X_LANES)
    pos = jnp.repeat(positions.reshape(T // ROPE_PACK, ROPE_PACK), ROPE_HALF, axis=1)
    rows = min(ROPE_TABLE_ROWS, T // ROPE_PACK)
    out = jax.ShapeDtypeStruct((T, ROPE_PAD), _F32)
    out_spec = pl.BlockSpec((rows * ROPE_PACK, ROPE_PAD), lambda i: (i, 0))
    cos, sin = pl.pallas_call(
        _rope_table_kernel,
        grid=(T // ROPE_PACK // rows,),
        in_specs=[pl.BlockSpec((rows, V7X_LANES), lambda i: (i, 0)),
                  pl.BlockSpec((1, V7X_LANES), lambda i: (0, 0))],
        out_specs=[out_spec, out_spec],
        out_shape=[out, out],
        compiler_params=pltpu.CompilerParams(dimension_semantics=("parallel",)),
        name="rope_tables",
    )(pos, invf)
    return cos.reshape(B, S, ROPE_PAD), sin.reshape(B, S, ROPE_PAD)


def _mla_proj_rows(r0, h_ref, cos_ref, sin_ref, gmix_ref, wdown_ref, gql_ref, gkvl_ref, wuq_ref,
                   wukv_ref, gq_ref, gkn_ref, gkr_ref, ones_ref, mq_ref, q_ref, k_ref, v_ref):
    rs = slice(r0, r0 + PROJ_SUB_ROWS)
    ones = ones_ref[...]
    h = h_ref[0, rs, :]
    r_h = lax.rsqrt(_row_mean_sq(h, ones, D_MODEL) + NORM_EPS)
    lat = _dot((h * gmix_ref[...]).astype(_BF16), wdown_ref[...])
    yield

    def latent_scale(x, n):
        return r_h * lax.rsqrt(r_h * r_h * _row_mean_sq(x, ones, n) + NORM_EPS)

    c_q = lat[:, :Q_LORA_RANK]
    c_kv = lat[:, Q_LORA_RANK:Q_LORA_RANK + KV_LORA_RANK]
    k_rope = lat[:, Q_LORA_RANK + KV_LORA_RANK:]
    r_cq = latent_scale(c_q, Q_LORA_RANK)
    c_q = (c_q * _lane_tile(r_cq, Q_LORA_RANK) * gql_ref[...]).astype(_BF16)
    r_ckv = latent_scale(c_kv, KV_LORA_RANK)
    c_kv = (c_kv * _lane_tile(r_ckv, KV_LORA_RANK) * gkvl_ref[...]).astype(_BF16)

    cos = cos_ref[0, rs, :]
    sin_signed = sin_ref[0, rs, :]

    def rope(x):
        return x * cos + pltpu.roll(x, ROPE_PAD // 2, 1) * sin_signed

    r_kr = latent_scale(k_rope, QK_ROPE_DIM)
    k_rope = rope(k_rope * r_kr * gkr_ref[...]).astype(_BF16)

    q = _dot(c_q, wuq_ref[...])
    kv = _dot(c_kv, wukv_ref[...])
    yield
    gq = gq_ref[...]
    gkn = gkn_ref[...]
    for hd in range(MLA_HEADS):
        xq = q[:, hd * QK_PAD_DIM:(hd + 1) * QK_PAD_DIM]
        yq = xq * lax.rsqrt(_mean_sq(xq, mq_ref[...]) + NORM_EPS) * gq
        q_ref[0, hd, rs, :QK_NOPE_DIM] = yq[:, :QK_NOPE_DIM].astype(_BF16)
        q_ref[0, hd, rs, QK_NOPE_DIM:] = rope(yq[:, QK_NOPE_DIM:]).astype(_BF16)
        k_ref[0, hd, rs, QK_NOPE_DIM:] = k_rope
        v0 = MLA_HEADS * QK_NOPE_DIM + hd * V_HEAD_DIM
        v_ref[0, hd, rs, :] = kv[:, v0:v0 + V_HEAD_DIM].astype(_BF16)
        xk = kv[:, hd * QK_NOPE_DIM:(hd + 1) * QK_NOPE_DIM]
        yk = (xk * _rms_scale(xk, QK_NOPE_DIM) * gkn).astype(_BF16)
        k_ref[0, hd, rs, :QK_NOPE_DIM] = yk
    yield


def _mla_proj_kernel(*refs):
    tiles = [_mla_proj_rows(r0, *refs) for r0 in range(0, PROJ_ROWS, PROJ_SUB_ROWS)]
    n_stages = 3
    for step in range(n_stages + len(tiles) - 1):
        for t, tile in enumerate(tiles):
            if 0 <= step - t < n_stages:
                next(tile)


def _mla_proj(h, cos, sin, layer_ops, shared_ops):
    B, S, D = h.shape
    rows = PROJ_ROWS
    H = MLA_HEADS
    ops = layer_ops + shared_ops
    resident = _slab_bytes([a for a, _ in ops])
    streamed = (_nbytes((rows, D), _F32) + 2 * _nbytes((rows, ROPE_PAD), _F32)
                + 2 * _nbytes((H, rows, QK_PAD_DIM), _BF16) + _nbytes((H, rows, V_HEAD_DIM), _BF16))
    scratch = 4 * _nbytes((rows, H * QK_PAD_DIM), _F32)
    return pl.pallas_call(
        _mla_proj_kernel,
        grid=(B, S // rows),
        in_specs=[pl.BlockSpec((1, rows, D), lambda b, i: (b, i, 0)),
                  pl.BlockSpec((1, rows, ROPE_PAD), lambda b, i: (b, i, 0)),
                  pl.BlockSpec((1, rows, ROPE_PAD), lambda b, i: (b, i, 0))]
                 + [_layer_spec(a, l) for a, l in ops],
        out_specs=[pl.BlockSpec((1, H, rows, QK_PAD_DIM), lambda b, i: (b, 0, i, 0)),
                   pl.BlockSpec((1, H, rows, QK_PAD_DIM), lambda b, i: (b, 0, i, 0)),
                   pl.BlockSpec((1, H, rows, V_HEAD_DIM), lambda b, i: (b, 0, i, 0))],
        out_shape=[jax.ShapeDtypeStruct((B, H, S, QK_PAD_DIM), _BF16),
                   jax.ShapeDtypeStruct((B, H, S, QK_PAD_DIM), _BF16),
                   jax.ShapeDtypeStruct((B, H, S, V_HEAD_DIM), _BF16)],
        compiler_params=pltpu.CompilerParams(
            dimension_semantics=("parallel", "parallel"),
            vmem_limit_bytes=_vmem_limit(resident, streamed, scratch)),
        name="mla_proj",
    )(h, cos, sin, *[a for a, _ in ops])


def _attn_kernel(q_ref, k_ref, v_ref, o_ref, vext_ref):
    S = q_ref.shape[2]
    tq = ATTN_Q_ROWS
    for hd in range(ATTN_HEADS_PER_STEP):
        vext_ref[hd, :, :V_HEAD_DIM] = v_ref[0, hd]
        vext_ref[hd, :, V_HEAD_DIM:] = jnp.ones((S, V_EXT_DIM - V_HEAD_DIM), _BF16)
    row = lax.broadcasted_iota(jnp.int32, (tq, tq), 0)
    col = lax.broadcasted_iota(jnp.int32, (tq, tq), 1)
    diag_mask = col <= row
    nt_dims = (((1,), (1,)), ((), ()))

    def scores(j, hd):
        q0, kend = j * tq, (j + 1) * tq
        q = q_ref[0, hd, q0:kend, :]
        s_diag = lax.dot_general(q, k_ref[0, hd, q0:kend, :], nt_dims, preferred_element_type=_F32)
        s_diag = jnp.where(diag_mask, s_diag, -jnp.inf)
        s_past = None
        if j:
            s_past = lax.dot_general(q, k_ref[0, hd, :q0, :], nt_dims, preferred_element_type=_F32)
        return s_past, s_diag

    def finish(j, hd, s_past, s_diag):
        q0, kend = j * tq, (j + 1) * tq
        m = jnp.max(s_diag, axis=-1, keepdims=True)
        o = None
        if j:
            m = jnp.maximum(m, jnp.max(s_past, axis=-1, keepdims=True))
            o = _dot(jnp.exp2(s_past - m).astype(_BF16), vext_ref[hd, :q0, :])
        o_diag = _dot(jnp.exp2(s_diag - m).astype(_BF16), vext_ref[hd, q0:kend, :])
        o = o_diag if o is None else o + o_diag
        o_ref[0, q0:kend, hd * V_HEAD_DIM:(hd + 1) * V_HEAD_DIM] = (
            o[:, :V_HEAD_DIM] / o[:, V_HEAD_DIM:]).astype(_BF16)

    units = [(j, hd) for j in range(S // tq) for hd in range(ATTN_HEADS_PER_STEP)]
    ahead = ATTN_SCORE_LOOKAHEAD
    pending = [scores(*u) for u in units[:ahead]]
    for i, unit in enumerate(units):
        if i + ahead < len(units):
            pending.append(scores(*units[i + ahead]))
        finish(*unit, *pending.pop(0))


def _attention(q, k, v):
    B, H, S, _ = q.shape
    hps = ATTN_HEADS_PER_STEP
    streamed = hps * (2 * _nbytes((S, QK_PAD_DIM), _BF16) + 2 * _nbytes((S, V_HEAD_DIM), _BF16))
    scratch = 8 * _nbytes((ATTN_Q_ROWS, S), _F32) + hps * _nbytes((S, V_EXT_DIM), _BF16)
    return pl.pallas_call(
        _attn_kernel,
        grid=(B, H // hps),
        in_specs=[pl.BlockSpec((1, hps, S, QK_PAD_DIM), lambda b, h: (b, h, 0, 0)),
                  pl.BlockSpec((1, hps, S, QK_PAD_DIM), lambda b, h: (b, h, 0, 0)),
                  pl.BlockSpec((1, hps, S, V_HEAD_DIM), lambda b, h: (b, h, 0, 0))],
        out_specs=pl.BlockSpec((1, S, hps * V_HEAD_DIM), lambda b, h: (b, 0, h)),
        out_shape=jax.ShapeDtypeStruct((B, S, H * V_HEAD_DIM), _BF16),
        scratch_shapes=[pltpu.VMEM((hps, S, V_EXT_DIM), _BF16)],
        compiler_params=pltpu.CompilerParams(
            dimension_semantics=("parallel", "parallel"),
            vmem_limit_bytes=_vmem_limit(0, streamed, scratch)),
        name="mla_attention",
    )(q, k, v)


def _post_kernel(n_casts, mix_ref, h_ref, p_ref, wmix_ref, gffn_ref, wup_ref, wdn_ref, gple_ref,
                 wgate_ref, wproj_ref, *rest):
    cast_in, out_ref, cast_out = rest[:n_casts], rest[n_casts], rest[n_casts + 1:]
    for src, dst in zip(cast_in, cast_out):
        dst[...] = src[...].astype(_BF16)
    h = h_ref[...] + _dot(mix_ref[...], wmix_ref[...])
    hn = (h * _rms_scale(h, D_MODEL) * gffn_ref[...]).astype(_BF16)
    for c in range(D_FF // FF_CHUNK):
        up = _dot(hn, wup_ref[:, c * FF_CHUNK:(c + 1) * FF_CHUNK])
        act = jnp.square(jnp.maximum(up, 0.0)).astype(_BF16)
        h = h + _dot(act, wdn_ref[c * FF_CHUNK:(c + 1) * FF_CHUNK, :])
    hn = (h * _rms_scale(h, D_MODEL) * gple_ref[...]).astype(_BF16)
    gate = jax.nn.sigmoid(_dot(hn, wgate_ref[...]))
    proj = _dot(p_ref[...].astype(_BF16), wproj_ref[...])
    out_ref[...] = h + gate * proj


def _post(mix, h, p, layer, ops, casts):
    T, D = h.shape
    rows = POST_ROWS
    steps = T // rows
    kmix = mix.shape[1]
    resident = _slab_bytes([a for a, _ in ops])
    cast_rows = [w.shape[1] // steps for w, _ in casts]
    streamed = (_nbytes((rows, kmix), _BF16) + 2 * _nbytes((rows, D), _F32)
                + _nbytes((rows, PLE_DIM), _F32)
                + sum(_nbytes((r, w.shape[2]), _F32) + _nbytes((r, w.shape[2]), _BF16)
                      for r, (w, _) in zip(cast_rows, casts)))
    scratch = 2 * _nbytes((rows, FF_CHUNK), _F32) + 4 * _nbytes((rows, D), _F32)

    def cast_spec(width, chunk, l):
        return pl.BlockSpec((None, chunk, width), lambda i: (l, i, 0))

    outs = pl.pallas_call(
        functools.partial(_post_kernel, len(casts)),
        grid=(steps,),
        in_specs=[pl.BlockSpec((rows, kmix), lambda i: (i, 0)),
                  pl.BlockSpec((rows, D), lambda i: (i, 0)),
                  pl.BlockSpec((None, rows, PLE_DIM), lambda i: (layer, i, 0))]
                 + [_layer_spec(a, l) for a, l in ops]
                 + [cast_spec(w.shape[2], r, l) for r, (w, l) in zip(cast_rows, casts)],
        out_specs=[pl.BlockSpec((rows, D), lambda i: (i, 0))]
                  + [cast_spec(w.shape[2], r, 0) for r, (w, _) in zip(cast_rows, casts)],
        out_shape=[jax.ShapeDtypeStruct((T, D), _F32)]
                  + [jax.ShapeDtypeStruct((1,) + w.shape[1:], _BF16) for w, _ in casts],
        compiler_params=pltpu.CompilerParams(
            dimension_semantics=("parallel",),
            vmem_limit_bytes=_vmem_limit(resident, streamed, scratch)),
        name="post_ffn_ple",
    )(mix, h, p, *[a for a, _ in ops], *[w for w, _ in casts])
    return outs[0], list(outs[1:])


_GELU_C = 0.7978845608028654
_GELU_A = 0.044715


def _gelu_tanh(x):
    half_x = 0.5 * x
    inner = x * (_GELU_C + (_GELU_C * _GELU_A) * (x * x))
    return half_x + half_x * jnp.tanh(inner)


def _gmlp_rows(row0, ws_masked, h_ref, gmix_ref, win_ref, lng_ref, lnb_ref, bst_ref, y_ref):
    h = h_ref[row0:row0 + GMLP_SUB_ROWS, :]
    hn = (h * _rms_scale(h, D_MODEL) * gmix_ref[...]).astype(_BF16)
    v = _gelu_tanh(_dot(hn, win_ref[:, GMLP_HALF:2 * GMLP_HALF]))
    mu = jnp.mean(v, axis=-1, keepdims=True)
    vc = v - mu
    var = jnp.mean(vc * vc, axis=-1, keepdims=True)
    vn = (vc * lax.rsqrt(var + NORM_EPS) * lng_ref[...] + lnb_ref[...]).astype(_BF16)
    u = _gelu_tanh(_dot(hn, win_ref[:, :GMLP_HALF]))
    for g in range(GMLP_GROUPS):
        bias = bst_ref[:, g:g + 1]
        c0 = g * GMLP_GROUP_DIM
        for n in range(GMLP_SUB_ROWS // GMLP_CHUNK):
            r0 = n * GMLP_CHUNK
            sv = _dot(ws_masked[g], vn[r0:r0 + GMLP_CHUNK, c0:c0 + GMLP_GROUP_DIM]) + bias
            y_ref[row0 + r0:row0 + r0 + GMLP_CHUNK, c0:c0 + GMLP_GROUP_DIM] = (
                u[r0:r0 + GMLP_CHUNK, c0:c0 + GMLP_GROUP_DIM] * sv).astype(_BF16)


def _gmlp_kernel(h_ref, gmix_ref, win_ref, lng_ref, lnb_ref, ws_ref, bst_ref, y_ref):
    t_idx = lax.broadcasted_iota(jnp.int32, (GMLP_CHUNK, GMLP_CHUNK), 0)
    s_idx = lax.broadcasted_iota(jnp.int32, (GMLP_CHUNK, GMLP_CHUNK), 1)
    causal = s_idx <= t_idx
    ws_masked = [jnp.where(causal, ws_ref[g], 0.0).astype(_BF16) for g in range(GMLP_GROUPS)]
    for row0 in range(0, GMLP_ROWS, GMLP_SUB_ROWS):
        _gmlp_rows(row0, ws_masked, h_ref, gmix_ref, win_ref, lng_ref, lnb_ref, bst_ref, y_ref)


def _gmlp_front(h, ops):
    T, D = h.shape
    rows = GMLP_ROWS
    resident = _slab_bytes([a for a, _ in ops])
    streamed = _nbytes((rows, D), _F32) + _nbytes((rows, GMLP_HALF), _BF16)
    scratch = 6 * _nbytes((rows, GMLP_HALF), _F32)
    return pl.pallas_call(
        _gmlp_kernel,
        grid=(T // rows,),
        in_specs=[pl.BlockSpec((rows, D), lambda i: (i, 0))] + [_layer_spec(a, l) for a, l in ops],
        out_specs=pl.BlockSpec((rows, GMLP_HALF), lambda i: (i, 0)),
        out_shape=jax.ShapeDtypeStruct((T, GMLP_HALF), _BF16),
        compiler_params=pltpu.CompilerParams(
            dimension_semantics=("parallel",),
            vmem_limit_bytes=_vmem_limit(resident, streamed, scratch)),
        name="gmlp_front",
    )(h, *[a for a, _ in ops])


def _gmlp_layer_rows(row0, ws_masked, h_ref, p_ref, gmix_ref, win_ref, lng_ref, lnb_ref, bst_ref,
                     wmix_ref, gffn_ref, wup_ref, wdn_ref, gple_ref, wgate_ref, wproj_ref, out_ref):
    rs = slice(row0, row0 + GMLP_SUB_ROWS)
    h = h_ref[rs, :]
    hn = (h * _rms_scale(h, D_MODEL) * gmix_ref[...]).astype(_BF16)
    v = _gelu_tanh(_dot(hn, win_ref[:, GMLP_HALF:2 * GMLP_HALF]))
    mu = jnp.mean(v, axis=-1, keepdims=True)
    vc = v - mu
    var = jnp.mean(vc * vc, axis=-1, keepdims=True)
    vn = (vc * lax.rsqrt(var + NORM_EPS) * lng_ref[...] + lnb_ref[...]).astype(_BF16)
    u = _gelu_tanh(_dot(hn, win_ref[:, :GMLP_HALF]))
    chunks = []
    for n in range(GMLP_SUB_ROWS // GMLP_CHUNK):
        r0 = n * GMLP_CHUNK
        groups = []
        for g in range(GMLP_GROUPS):
            c0 = g * GMLP_GROUP_DIM
            sv = _dot(ws_masked[g], vn[r0:r0 + GMLP_CHUNK, c0:c0 + GMLP_GROUP_DIM]) + bst_ref[:, g:g + 1]
            groups.append((u[r0:r0 + GMLP_CHUNK, c0:c0 + GMLP_GROUP_DIM] * sv).astype(_BF16))
        chunks.append(jnp.concatenate(groups, axis=1))
    y = jnp.concatenate(chunks, axis=0)
    yield
    h = h + _dot(y, wmix_ref[...])
    hn = (h * _rms_scale(h, D_MODEL) * gffn_ref[...]).astype(_BF16)
    for c in range(D_FF // FF_CHUNK):
        up = _dot(hn, wup_ref[:, c * FF_CHUNK:(c + 1) * FF_CHUNK])
        act = jnp.square(jnp.maximum(up, 0.0)).astype(_BF16)
        h = h + _dot(act, wdn_ref[c * FF_CHUNK:(c + 1) * FF_CHUNK, :])
    hn = (h * _rms_scale(h, D_MODEL) * gple_ref[...]).astype(_BF16)
    gate = jax.nn.sigmoid(_dot(hn, wgate_ref[...]))
    proj = _dot(p_ref[rs, :].astype(_BF16), wproj_ref[...])
    out_ref[rs, :] = h + gate * proj
    yield


def _gmlp_layer_kernel(n_casts, h_ref, p_ref, gmix_ref, win_ref, lng_ref, lnb_ref, ws_ref, bst_ref,
                       wmix_ref, gffn_ref, wup_ref, wdn_ref, gple_ref, wgate_ref, wproj_ref, *rest):
    cast_in, out_ref, cast_out = rest[:n_casts], rest[n_casts], rest[n_casts + 1:]
    for src, dst in zip(cast_in, cast_out):
        dst[...] = src[...].astype(_BF16)
    t_idx = lax.broadcasted_iota(jnp.int32, (GMLP_CHUNK, GMLP_CHUNK), 0)
    s_idx = lax.broadcasted_iota(jnp.int32, (GMLP_CHUNK, GMLP_CHUNK), 1)
    ws_masked = [jnp.where(s_idx <= t_idx, ws_ref[g], 0.0).astype(_BF16) for g in range(GMLP_GROUPS)]
    tiles = [_gmlp_layer_rows(row0, ws_masked, h_ref, p_ref, gmix_ref, win_ref, lng_ref, lnb_ref,
                              bst_ref, wmix_ref, gffn_ref, wup_ref, wdn_ref, gple_ref, wgate_ref,
                              wproj_ref, out_ref)
             for row0 in range(0, GMLP_LAYER_ROWS, GMLP_SUB_ROWS)]
    n_stages = 2
    for step in range(n_stages + len(tiles) - 1):
        for t, tile in enumerate(tiles):
            if 0 <= step - t < n_stages:
                next(tile)


def _gmlp_layer(h, p, layer, front_ops, post_ops, casts):
    T, D = h.shape
    rows = GMLP_LAYER_ROWS
    steps = T // rows
    ops = front_ops + post_ops
    resident = _slab_bytes([a for a, _ in ops])
    cast_rows = [w.shape[1] // steps for w, _ in casts]
    streamed = (2 * _nbytes((rows, D), _F32) + _nbytes((rows, PLE_DIM), _F32)
                + sum(_nbytes((r, w.shape[2]), _F32) + _nbytes((r, w.shape[2]), _BF16)
                      for r, (w, _) in zip(cast_rows, casts)))
    scratch = 5 * _nbytes((rows, GMLP_HALF), _F32)

    def cast_spec(width, chunk, l):
        return pl.BlockSpec((None, chunk, width), lambda i: (l, i, 0))

    outs = pl.pallas_call(
        functools.partial(_gmlp_layer_kernel, len(casts)),
        grid=(steps,),
        in_specs=[pl.BlockSpec((rows, D), lambda i: (i, 0)),
                  pl.BlockSpec((None, rows, PLE_DIM), lambda i: (layer, i, 0))]
                 + [_layer_spec(a, l) for a, l in ops]
                 + [cast_spec(w.shape[2], r, l) for r, (w, l) in zip(cast_rows, casts)],
        out_specs=[pl.BlockSpec((rows, D), lambda i: (i, 0))]
                  + [cast_spec(w.shape[2], r, 0) for r, (w, _) in zip(cast_rows, casts)],
        out_shape=[jax.ShapeDtypeStruct((T, D), _F32)]
                  + [jax.ShapeDtypeStruct((1,) + w.shape[1:], _BF16) for w, _ in casts],
        compiler_params=pltpu.CompilerParams(
            dimension_semantics=("parallel",),
            vmem_limit_bytes=_vmem_limit(resident, streamed, scratch)),
        name="gmlp_layer",
    )(h, p, *[a for a, _ in ops], *[w for w, _ in casts])
    return outs[0], list(outs[1:])


def _rope_pad(w):
    z = jnp.zeros(w.shape[:-1] + (ROPE_HALF,), w.dtype)
    return jnp.concatenate([w[..., :ROPE_HALF], z, w[..., ROPE_HALF:], z], axis=-1)


def _skew_pitch_kernel(w_ref, out_ref):
    n = w_ref.shape[-1]
    out_ref[:, :n] = w_ref[...].astype(_BF16)
    out_ref[:, n:] = jnp.zeros((out_ref.shape[0], out_ref.shape[1] - n), _BF16)


def _skew_pitch(w):
    L, K, N = w.shape
    rows = SKEW_COPY_ROWS
    return pl.pallas_call(
        _skew_pitch_kernel,
        grid=(L, K // rows),
        in_specs=[pl.BlockSpec((None, rows, N), lambda l, i: (l, i, 0))],
        out_specs=pl.BlockSpec((None, rows, N + V7X_LANES), lambda l, i: (l, i, 0)),
        out_shape=jax.ShapeDtypeStruct((L, K, N + V7X_LANES), _BF16),
        compiler_params=pltpu.CompilerParams(dimension_semantics=("parallel", "parallel")),
        name="skew_pitch",
    )(w)


def _rows(g):
    return g.reshape(g.shape[0], 1, g.shape[-1]).astype(_F32)


def _mla_params(w_down, w_uq, w_ukv, q_nope_g, q_rope_g, k_nope_g, k_rope_g):
    H = MLA_HEADS
    L = w_down.shape[0]
    n_lat = Q_LORA_RANK + KV_LORA_RANK
    wdown = jnp.concatenate([w_down[..., :n_lat], _rope_pad(w_down[..., n_lat:])], axis=-1)
    uq = w_uq.reshape(L, Q_LORA_RANK, H, QK_NOPE_DIM + QK_ROPE_DIM)
    wuq = jnp.concatenate([uq[..., :QK_NOPE_DIM], _rope_pad(uq[..., QK_NOPE_DIM:])], axis=-1)
    wuq = wuq.reshape(L, Q_LORA_RANK, H * QK_PAD_DIM)
    ukv = w_ukv.reshape(L, KV_LORA_RANK, H, QK_NOPE_DIM + V_HEAD_DIM)
    wukv = jnp.concatenate([ukv[..., :QK_NOPE_DIM].reshape(L, KV_LORA_RANK, H * QK_NOPE_DIM),
                            ukv[..., QK_NOPE_DIM:].reshape(L, KV_LORA_RANK, H * V_HEAD_DIM)], axis=-1)
    q_scale = (QK_NOPE_DIM + QK_ROPE_DIM) ** -0.5 * math.log2(math.e)
    gq = _rows(jnp.concatenate([q_nope_g, _rope_pad(q_rope_g)], axis=-1)) * q_scale
    gkn = _rows(k_nope_g)
    return (wdown.astype(_BF16), wuq.astype(_BF16), wukv.astype(_BF16), gq, gkn,
            _rows(_rope_pad(k_rope_g)))


def _group_mean_matrix(group_sizes, counts):
    blocks = []
    width = sum(group_sizes)
    off = 0
    for size, count in zip(group_sizes, counts):
        col = jnp.zeros((size, width), _F32).at[:, off:off + size].set(1.0 / count)
        blocks.append(col)
        off += size
    return jnp.concatenate(blocks, axis=0).astype(_BF16)[None]


def _stat_consts():
    ones = jnp.ones((1, V7X_LANES, V7X_LANES), _BF16)
    mq = _group_mean_matrix((QK_NOPE_DIM, ROPE_PAD), (QK_NOPE_DIM, QK_ROPE_DIM))
    return [(c, 0) for c in (ones, mq)]


def kernel(x, p, positions, norm_mix, norm_ffn, norm_ple, mla_w_down, mla_q_lora_g, mla_kv_lora_g,
           mla_w_uq, mla_w_ukv, mla_q_nope_g, mla_q_rope_g, mla_k_nope_g, mla_k_rope_g, mla_w_out,
           gmlp_w_in, gmlp_ln_g, gmlp_ln_b, gmlp_w_s, gmlp_b_s, gmlp_w_out, ffn_w_up, ffn_w_down,
           ple_w_gate, ple_w_proj):
    B, S, D = x.shape
    T = B * S
    p_flat = p.reshape(DEPTH, T, PLE_DIM)
    g_mix, g_ffn, g_ple = _rows(norm_mix), _rows(norm_ffn), _rows(norm_ple)
    wdown, wuq, wukv, gq, gkn, gkr = _mla_params(mla_w_down, mla_w_uq, mla_w_ukv, mla_q_nope_g,
                                                 mla_q_rope_g, mla_k_nope_g, mla_k_rope_g)
    gql, gkvl = _rows(mla_q_lora_g), _rows(mla_kv_lora_g)
    gmlp_win = _skew_pitch(gmlp_w_in)
    ln_g, ln_b = _rows(gmlp_ln_g), _rows(gmlp_ln_b)
    bst = jnp.swapaxes(gmlp_b_s, 1, 2)
    w_proj = ple_w_proj.astype(_BF16)
    shared = _stat_consts()
    cos, sin = _rope_tables(positions)

    def post_weights_f32(i):
        mix_w = (mla_w_out, i // 2) if i % 2 == 0 else (gmlp_w_out, i // 2)
        return [mix_w, (ffn_w_up, i), (ffn_w_down, i), (ple_w_gate, i)]

    post_w = [w[l:l + 1].astype(_BF16) for w, l in post_weights_f32(0)]

    h = x.reshape(T, D)
    for i in range(DEPTH):
        j = i // 2
        if i % 2 == 0:
            layer_ops = [(g_mix, i), (wdown, j), (gql, j), (gkvl, j), (wuq, j), (wukv, j), (gq, j),
                         (gkn, j), (gkr, j)]
            q, k, v = _mla_proj(h.reshape(B, S, D), cos, sin, layer_ops, shared)
            mix = _attention(q, k, v).reshape(T, MLA_HEADS * V_HEAD_DIM)
        wmix, w_up, w_dn, w_gate = post_w
        casts = post_weights_f32(i + 1) if i + 1 < DEPTH else []
        post_ops = [(wmix, 0), (g_ffn, i), (w_up, 0), (w_dn, 0), (g_ple, i), (w_gate, 0),
                    (w_proj, i)]
        if i % 2 == 0:
            h, post_w = _post(mix, h, p_flat, i, post_ops, casts)
        else:
            front_ops = [(g_mix, i), (gmlp_win, j), (ln_g, j), (ln_b, j), (gmlp_w_s, j), (bst, j)]
            h, post_w = _gmlp_layer(h, p_flat, i, front_ops, post_ops, casts)
    return h.reshape(B, S, D)
```
